```python
import jax, jax.numpy as jnp
from jax import lax
import numpy as np

D_MODEL = 1024
BATCH = 2
SEQ = 16384
DEPTH = 4
DEC_BATCH = 16
DEC_SEQ = 16
PAST_LEN = 4096

CHUNK = 64
D_MIX = D_MODEL
D_CONV = D_MIX // 2
D_LRU = D_MIX - D_CONV
CONV_W = 31
CONV_GROUPS = 8
LRU_HEADS = 8
LRU_HD = D_LRU // LRU_HEADS
LRU_CONV_W = 4
LRU_C = 8.0
D_FF = 2816
FFN_CONV_W = 3
ALPHA = (2 * DEPTH) ** 0.25
BETA = (8 * DEPTH) ** -0.25
LN_EPS = 1e-5

kernel_name = "hybrid_conformer_rglru_streaming_step"


def layer_norm(x, g, b):
    xf = x.astype(jnp.float32)
    mu = jnp.mean(xf, axis=-1, keepdims=True)
    var = jnp.mean(jnp.square(xf - mu), axis=-1, keepdims=True)
    return ((xf - mu) * lax.rsqrt(var + LN_EPS) * g + b).astype(x.dtype)


def group_norm_per_frame(x, g, b, groups):
    shp = x.shape
    xf = x.astype(jnp.float32).reshape(shp[:-1] + (groups, shp[-1] // groups))
    mu = jnp.mean(xf, axis=-1, keepdims=True)
    var = jnp.mean(jnp.square(xf - mu), axis=-1, keepdims=True)
    xn = ((xf - mu) * lax.rsqrt(var + LN_EPS)).reshape(shp)
    return (xn * g + b).astype(x.dtype)


def causal_dwconv(x, prev, w, b):
    width = w.shape[0]
    xp = jnp.concatenate([prev.astype(x.dtype), x], axis=1)
    y = lax.conv_general_dilated(
        xp, w[:, None, :].astype(x.dtype), window_strides=(1,), padding='VALID',
        dimension_numbers=('NWC', 'WIO', 'NWC'), feature_group_count=x.shape[-1])
    return y + b.astype(x.dtype), xp[:, xp.shape[1] - (width - 1):]


def _lin_combine(left, right):
    a1, b1 = left
    a2, b2 = right
    return a1 * a2, a2 * b1 + b2


def rg_lru(x, h0, wa, ba, wx, bx, lam, start_pos):
    B, T, _ = x.shape
    xf = x.astype(jnp.float32)
    xh = xf.reshape(B, T, LRU_HEADS, LRU_HD)
    r = jax.nn.sigmoid(jnp.einsum('bthd,hde->bthe', xh, wa.astype(jnp.float32))
                       + ba.astype(jnp.float32)).reshape(B, T, D_LRU)
    i = jax.nn.sigmoid(jnp.einsum('bthd,hde->bthe', xh, wx.astype(jnp.float32))
                       + bx.astype(jnp.float32)).reshape(B, T, D_LRU)
    log_a = -LRU_C * r * jax.nn.softplus(-lam.astype(jnp.float32))
    a = jnp.exp(log_a)
    mult = jnp.sqrt(-jnp.expm1(2.0 * log_a))
    reset = ((start_pos + jnp.arange(T)) == 0)[None, :, None]
    mult = jnp.where(reset, 1.0, mult)
    a = jnp.where(reset, 0.0, a)
    bterm = mult * (i * xf)
    bterm = bterm.at[:, 0].add(a[:, 0] * h0.astype(jnp.float32))
    _, h = lax.associative_scan(_lin_combine, (a, bterm), axis=1)
    return h.astype(x.dtype), h[:, -1].astype(x.dtype)


def trunk_layer(x, st_cdw, st_lconv, st_h, st_fconv, start_pos, p):
    z = x @ p['w_in'] + p['b_in']
    za, zg, zx, zgate = jnp.split(z, [D_CONV, 2 * D_CONV, 2 * D_CONV + D_LRU], axis=-1)
    glu = za * jax.nn.sigmoid(zg)
    c, new_cdw = causal_dwconv(glu, st_cdw, p['conv_dw_w'], p['conv_dw_b'])
    c = jax.nn.silu(group_norm_per_frame(c, p['conv_gn_g'], p['conv_gn_b'], CONV_GROUPS))
    xl, new_lconv = causal_dwconv(zx, st_lconv, p['lru_conv_w'], p['lru_conv_b'])
    hseq, h_last = rg_lru(xl, st_h, p['lru_wa'], p['lru_ba'], p['lru_wx'], p['lru_bx'],
                          p['lru_lambda'], start_pos)
    yl = jax.nn.gelu(zgate) * hseq
    mix = jnp.concatenate([c, yl], axis=-1) @ p['w_out'] + p['b_out']
    x = layer_norm(ALPHA * x + mix, p['ln1_g'], p['ln1_b'])
    u = x @ p['ffn_w_up']
    v, g = jnp.split(u, [D_FF], axis=-1)
    v, new_fconv = causal_dwconv(v, st_fconv, p['ffn_conv_w'], p['ffn_conv_b'])
    ffn = (jax.nn.gelu(v) * g) @ p['ffn_w_down']
    x = layer_norm(ALPHA * x + ffn, p['ln2_g'], p['ln2_b'])
    return x, new_cdw, new_lconv, h_last, new_fconv


def setup_inputs(seed: int = 0) -> dict:
    key = jax.random.key(seed)
    ks = iter(jax.random.split(key, 40))
    nrm = lambda shape, s: jax.random.normal(next(ks), shape, jnp.float32) * s
    a_init = jax.random.uniform(next(ks), (DEPTH, D_LRU), jnp.float32, 0.9, 0.999)
    return {
        'x_prompt': nrm((BATCH, SEQ, D_MODEL), 1.0),
        'x_sample': nrm((DEC_BATCH, DEC_SEQ, D_MODEL), 1.0),
        'state_conv_dw': nrm((DEPTH, DEC_BATCH, CONV_W - 1, D_CONV), 0.5),
        'state_lru_conv': nrm((DEPTH, DEC_BATCH, LRU_CONV_W - 1, D_LRU), 1.0),
        'state_lru_h': nrm((DEPTH, DEC_BATCH, D_LRU), 0.5),
        'state_ffn_conv': nrm((DEPTH, DEC_BATCH, FFN_CONV_W - 1, D_FF), 1.0),
        'ln0_g': 1.0 + nrm((D_MODEL,), 0.01),
        'ln0_b': nrm((D_MODEL,), 0.01),
        'w_in': nrm((DEPTH, D_MODEL, 2 * D_MIX), D_MODEL ** -0.5),
        'b_in': nrm((DEPTH, 2 * D_MIX), 0.01),
        'conv_dw_w': nrm((DEPTH, CONV_W, D_CONV), CONV_W ** -0.5),
        'conv_dw_b': nrm((DEPTH, D_CONV), 0.01),
        'conv_gn_g': 1.0 + nrm((DEPTH, D_CONV), 0.01),
        'conv_gn_b': nrm((DEPTH, D_CONV), 0.01),
        'lru_conv_w': nrm((DEPTH, LRU_CONV_W, D_LRU), LRU_CONV_W ** -0.5),
        'lru_conv_b': nrm((DEPTH, D_LRU), 0.01),
        'lru_wa': nrm((DEPTH, LRU_HEADS, LRU_HD, LRU_HD), LRU_HD ** -0.5),
        'lru_ba': nrm((DEPTH, LRU_HEADS, LRU_HD), 0.01),
        'lru_wx': nrm((DEPTH, LRU_HEADS, LRU_HD, LRU_HD), LRU_HD ** -0.5),
        'lru_bx': nrm((DEPTH, LRU_HEADS, LRU_HD), 0.01),
        'lru_lambda': jnp.log(a_init) - jnp.log1p(-a_init),
        'w_out': nrm((DEPTH, D_MIX, D_MODEL), BETA * D_MIX ** -0.5),
        'b_out': nrm((DEPTH, D_MODEL), 0.01),
        'ln1_g': 1.0 + nrm((DEPTH, D_MODEL), 0.01),
        'ln1_b': nrm((DEPTH, D_MODEL), 0.01),
        'ffn_w_up': nrm((DEPTH, D_MODEL, 2 * D_FF), D_MODEL ** -0.5),
        'ffn_conv_w': nrm((DEPTH, FFN_CONV_W, D_FF), FFN_CONV_W ** -0.5),
        'ffn_conv_b': nrm((DEPTH, D_FF), 0.01),
        'ffn_w_down': nrm((DEPTH, D_FF, D_MODEL), BETA * D_FF ** -0.5),
        'ln2_g': 1.0 + nrm((DEPTH, D_MODEL), 0.01),
        'ln2_b': nrm((DEPTH, D_MODEL), 0.01),
    }


def reference(x_prompt, x_sample, state_conv_dw, state_lru_conv, state_lru_h, state_ffn_conv,
              ln0_g, ln0_b, w_in, b_in, conv_dw_w, conv_dw_b, conv_gn_g, conv_gn_b,
              lru_conv_w, lru_conv_b, lru_wa, lru_ba, lru_wx, lru_bx, lru_lambda,
              w_out, b_out, ln1_g, ln1_b, ffn_w_up, ffn_conv_w, ffn_conv_b, ffn_w_down,
              ln2_g, ln2_b):
    xp = layer_norm(x_prompt, ln0_g, ln0_b)
    xs = layer_norm(x_sample, ln0_g, ln0_b)
    nb = x_prompt.shape[0]
    dt = x_prompt.dtype
    zero_cdw = jnp.zeros((nb, CONV_W - 1, D_CONV), dt)
    zero_lconv = jnp.zeros((nb, LRU_CONV_W - 1, D_LRU), dt)
    zero_h = jnp.zeros((nb, D_LRU), dt)
    zero_fconv = jnp.zeros((nb, FFN_CONV_W - 1, D_FF), dt)
    p_cdw, p_lc, p_h, p_fc = [], [], [], []
    s_cdw, s_lc, s_h, s_fc = [], [], [], []
    for l in range(DEPTH):
        p = dict(w_in=w_in[l], b_in=b_in[l], conv_dw_w=conv_dw_w[l], conv_dw_b=conv_dw_b[l],
                 conv_gn_g=conv_gn_g[l], conv_gn_b=conv_gn_b[l],
                 lru_conv_w=lru_conv_w[l], lru_conv_b=lru_conv_b[l],
                 lru_wa=lru_wa[l], lru_ba=lru_ba[l],
                 lru_wx=lru_wx[l], lru_bx=lru_bx[l], lru_lambda=lru_lambda[l],
                 w_out=w_out[l], b_out=b_out[l], ln1_g=ln1_g[l], ln1_b=ln1_b[l],
                 ffn_w_up=ffn_w_up[l], ffn_conv_w=ffn_conv_w[l], ffn_conv_b=ffn_conv_b[l],
                 ffn_w_down=ffn_w_down[l], ln2_g=ln2_g[l], ln2_b=ln2_b[l])
        xp, c1, c2, c3, c4 = trunk_layer(xp, zero_cdw, zero_lconv, zero_h, zero_fconv, 0, p)
        p_cdw.append(c1); p_lc.append(c2); p_h.append(c3); p_fc.append(c4)
        xs, d1, d2, d3, d4 = trunk_layer(xs, state_conv_dw[l], state_lru_conv[l], state_lru_h[l],
                                         state_ffn_conv[l], PAST_LEN, p)
        s_cdw.append(d1); s_lc.append(d2); s_h.append(d3); s_fc.append(d4)
    return (xp, xs,
            jnp.stack(p_cdw), jnp.stack(p_lc), jnp.stack(p_h), jnp.stack(p_fc),
            jnp.stack(s_cdw), jnp.stack(s_lc), jnp.stack(s_h), jnp.stack(s_fc))
```

```python
import functools

import jax
import jax.numpy as jnp
from jax import lax
from jax.experimental import pallas as pl
from jax.experimental.pallas import tpu as pltpu

D_MODEL = 1024
D_CONV = 512
D_LRU = 512
CONV_W = 31
LRU_CONV_W = 4
FFN_CONV_W = 3
CONV_GROUPS = 8
LRU_HEADS = 8
LRU_C = 8.0
D_FF = 2816
DEPTH = 4
PAST_LEN = 4096
ALPHA = (2 * DEPTH) ** 0.25
LN_EPS = 1e-5

SUBLANES = 8
LANES = 128
MXU_DIM = 256
VMEM_LIMIT_BYTES = 60000 * 1024

FF_CHUNK = MXU_DIM
N_FF = D_FF // FF_CHUNK
HIST_DW = 32
HIST_SMALL = SUBLANES
PROMPT_TILE = 512


def _rows(j, n=SUBLANES):
    if isinstance(j, int):
        return pl.ds(j * n, n)
    return pl.ds(pl.multiple_of(j * n, n), n)


def _loop(n, body, init):
    if n <= 4:
        carry = init
        for j in range(n):
            carry = body(j, carry)
        return carry
    return lax.fori_loop(0, n, body, init)


def _layer_norm(x, g, b):
    mu = jnp.mean(x, axis=-1, keepdims=True)
    d = x - mu
    var = jnp.mean(d * d, axis=-1, keepdims=True)
    return d * lax.rsqrt(var + LN_EPS) * g + b


def _bdot(a, w):
    return jnp.dot(a, w, preferred_element_type=jnp.float32)


def _split_bf16(v):
    hi = v.astype(jnp.bfloat16)
    lo = (v - hi.astype(jnp.float32)).astype(jnp.bfloat16)
    return hi, lo


def _group_mean(v, gmat):
    hi, lo = _split_bf16(v)
    halves = []
    for h in range(D_CONV // MXU_DIM):
        cs = slice(h * MXU_DIM, (h + 1) * MXU_DIM)
        halves.append(_bdot(hi[:, cs], gmat) + _bdot(lo[:, cs], gmat))
    return jnp.concatenate(halves, axis=1)


def _causal_conv(buf, seg, get_w, bias, width, hist, n_blocks, n_ch, out_ref, out_row0):
    base = hist - (width - 1)
    taps = {}
    for k in range(width):
        o = base + k
        taps.setdefault(o % SUBLANES, []).append((o // SUBLANES, k))
    rs = sorted(taps)
    sub = lax.broadcasted_iota(jnp.int32, (SUBLANES, LANES), 0)

    for c0 in range(0, n_ch, LANES):
        cs = slice(c0, c0 + LANES)

        def partials(jb):
            rows = {}
            out = []
            for r in rs:
                acc = None
                for (q, k) in taps[r]:
                    if q not in rows:
                        rows[q] = buf[seg, _rows(jb + q), cs]
                    term = get_w(k, cs) * rows[q]
                    acc = term if acc is None else acc + term
                out.append(acc)
            return tuple(out)

        def body(j, p_cur):
            p_next = partials(j + 1)
            y = bias[:, cs]
            for idx, r in enumerate(rs):
                if r == 0:
                    y = y + p_cur[idx]
                else:
                    sel = jnp.where(sub >= r, p_cur[idx], p_next[idx])
                    y = y + pltpu.roll(sel, SUBLANES - r, 0)
            if isinstance(j, int):
                out_ref[pl.ds(out_row0 + j * SUBLANES, SUBLANES), cs] = y
            else:
                out_ref[pl.ds(pl.multiple_of(out_row0 + j * SUBLANES, SUBLANES), SUBLANES), cs] = y
            return p_next

        _loop(n_blocks, body, partials(0))


def _lru_scan(a_ref, b_ref, h_bcast, row0, n_blocks):
    sub = lax.broadcasted_iota(jnp.int32, (SUBLANES, D_LRU), 0)

    def body(j, h_prev):
        if isinstance(j, int):
            rows = pl.ds(row0 + j * SUBLANES, SUBLANES)
        else:
            rows = pl.ds(pl.multiple_of(row0 + j * SUBLANES, SUBLANES), SUBLANES)
        a = a_ref[rows, :]
        b = b_ref[rows, :]
        for sft in (1, 2, 4):
            a_sh = jnp.where(sub >= sft, pltpu.roll(a, sft, 0), 1.0)
            b_sh = jnp.where(sub >= sft, pltpu.roll(b, sft, 0), 0.0)
            b = a * b_sh + b
            a = a * a_sh
        h = b + a * h_prev
        b_ref[rows, :] = h
        return jnp.broadcast_to(h[SUBLANES - 1:SUBLANES, :], (SUBLANES, D_LRU))

    return _loop(n_blocks, body, h_bcast)


def _layer_kernel(
        nseg, seg_len, start_pos, apply_ln0,
        x_ref, st_cdw, st_lc, st_h, st_fc, ln0_g, ln0_b,
        w_in, b_in, cw, cb, gn_g, gn_b, gmat,
        lw, lb, wa, wx, ba, bx, lam,
        w_out, b_out, ln1_g, ln1_b,
        w_up, fw, fb, w_down, ln2_g, ln2_b,
        y_ref, o_cdw, o_lc, o_h, o_fc,
        cbuf, lbuf, fbuf, fhist, hst, xs, xb_s, c_s, xl_s, a_s, b_s, mix_s, x1_s, x1b_s, v2_s, acc_s):
    t = pl.program_id(1)
    n_t = pl.num_programs(1)
    n_blocks = seg_len // SUBLANES
    f32 = jnp.float32
    bf16 = jnp.bfloat16

    @pl.when(t == 0)
    def _():
        for s in range(nseg):
            cbuf[s, 0:HIST_DW - (CONV_W - 1), :] = jnp.zeros((HIST_DW - (CONV_W - 1), D_CONV), f32)
            cbuf[s, HIST_DW - (CONV_W - 1):HIST_DW, :] = st_cdw[s]
            cbuf[s, HIST_DW + seg_len:HIST_DW + seg_len + SUBLANES, :] = jnp.zeros((SUBLANES, D_CONV), f32)
            lbuf[s, 0:HIST_SMALL - (LRU_CONV_W - 1), :] = jnp.zeros((HIST_SMALL - (LRU_CONV_W - 1), D_LRU), f32)
            lbuf[s, HIST_SMALL - (LRU_CONV_W - 1):HIST_SMALL, :] = st_lc[s]
            lbuf[s, HIST_SMALL + seg_len:HIST_SMALL + seg_len + SUBLANES, :] = jnp.zeros((SUBLANES, D_LRU), f32)
            fbuf[s, HIST_SMALL + seg_len:HIST_SMALL + seg_len + SUBLANES, :] = jnp.zeros((SUBLANES, FF_CHUNK), f32)
            hst[s] = jnp.broadcast_to(st_h[s], (SUBLANES, D_LRU))
            for c in range(N_FF):
                fhist[c, s, 0:HIST_SMALL - (FFN_CONV_W - 1), :] = jnp.zeros(
                    (HIST_SMALL - (FFN_CONV_W - 1), FF_CHUNK), f32)
                fhist[c, s, HIST_SMALL - (FFN_CONV_W - 1):HIST_SMALL, :] = st_fc[s, :, c * FF_CHUNK:(c + 1) * FF_CHUNK]

    x = x_ref[0]
    if apply_ln0:
        x = _layer_norm(x, ln0_g[...], ln0_b[...])
    xs[...] = x
    xb_s[...] = x.astype(bf16)

    za = _bdot(xb_s[...], w_in[:, 0:D_CONV]) + b_in[:, 0:D_CONV]
    zg = _bdot(xb_s[...], w_in[:, D_CONV:2 * D_CONV]) + b_in[:, D_CONV:2 * D_CONV]
    glu = za * jax.nn.sigmoid(zg)
    for s in range(nseg):
        cbuf[s, HIST_DW:HIST_DW + seg_len, :] = glu[s * seg_len:(s + 1) * seg_len]
    cb_b = jnp.broadcast_to(cb[...], (SUBLANES, D_CONV))
    for s in range(nseg):
        _causal_conv(cbuf, s, lambda k, cs: cw[k, :, cs], cb_b, CONV_W, HIST_DW, n_blocks, D_CONV,
                     c_s, s * seg_len)
        cbuf[s, 0:HIST_DW, :] = cbuf[s, seg_len:seg_len + HIST_DW, :]
    c = c_s[...]
    d = c - _group_mean(c, gmat[...])
    var = _group_mean(d * d, gmat[...])
    cn = d * lax.rsqrt(var + LN_EPS) * gn_g[...] + gn_b[...]
    c_act = cn * jax.nn.sigmoid(cn)
    mix_s[...] = _bdot(c_act.astype(bf16), w_out[0:D_CONV, :]) + b_out[...]

    zx = _bdot(xb_s[...], w_in[:, 2 * D_CONV:2 * D_CONV + D_LRU]) + b_in[:, 2 * D_CONV:2 * D_CONV + D_LRU]
    for s in range(nseg):
        lbuf[s, HIST_SMALL:HIST_SMALL + seg_len, :] = zx[s * seg_len:(s + 1) * seg_len]
    lb_b = jnp.broadcast_to(lb[...], (SUBLANES, D_LRU))
    for s in range(nseg):
        _causal_conv(lbuf, s, lambda k, cs: lw[k, :, cs], lb_b, LRU_CONV_W, HIST_SMALL, n_blocks, D_LRU,
                     xl_s, s * seg_len)
        lbuf[s, 0:HIST_SMALL, :] = lbuf[s, seg_len:seg_len + HIST_SMALL, :]
    xl = xl_s[...]
    xlb = xl.astype(bf16)
    ra, ix = [], []
    for h in range(D_LRU // MXU_DIM):
        cs = slice(h * MXU_DIM, (h + 1) * MXU_DIM)
        ra.append(_bdot(xlb[:, cs], wa[h]))
        ix.append(_bdot(xlb[:, cs], wx[h]))
    r = jax.nn.sigmoid(jnp.concatenate(ra, axis=1) + ba[...])
    i = jax.nn.sigmoid(jnp.concatenate(ix, axis=1) + bx[...])
    log_a = r * (-LRU_C * jax.nn.softplus(-lam[...]))
    a = jnp.exp(log_a)
    mult = jnp.sqrt(-jnp.tanh(log_a) * (a * a + 1.0))
    if start_pos == 0:
        row = lax.broadcasted_iota(jnp.int32, (nseg * seg_len, D_LRU), 0)
        if nseg > 1:
            row = row % seg_len
        reset = (row + t * seg_len) == 0
        mult = jnp.where(reset, 1.0, mult)
        a = jnp.where(reset, 0.0, a)
    a_s[...] = a
    b_s[...] = mult * (i * xl)
    for s in range(nseg):
        hst[s] = _lru_scan(a_s, b_s, hst[s], s * seg_len, n_blocks)
    zgate = _bdot(xb_s[...], w_in[:, 2 * D_CONV + D_LRU:]) + b_in[:, 2 * D_CONV + D_LRU:]
    yl = jax.nn.gelu(zgate) * b_s[...]
    mix = mix_s[...] + _bdot(yl.astype(bf16), w_out[D_CONV:, :])
    x1 = _layer_norm(ALPHA * xs[...] + mix, ln1_g[...], ln1_b[...])
    x1_s[...] = x1
    x1b_s[...] = x1.astype(bf16)

    acc_s[...] = jnp.zeros_like(acc_s)

    def ffn_chunk(c, carry):
        v = _bdot(x1b_s[...], w_up[c])
        g = _bdot(x1b_s[...], w_up[N_FF + c])
        fb_b = jnp.broadcast_to(fb[c], (SUBLANES, FF_CHUNK))
        for s in range(nseg):
            fbuf[s, 0:HIST_SMALL, :] = fhist[c, s]
            fbuf[s, HIST_SMALL:HIST_SMALL + seg_len, :] = v[s * seg_len:(s + 1) * seg_len]
            _causal_conv(fbuf, s, lambda k, cs: fw[c, k, :, cs], fb_b, FFN_CONV_W, HIST_SMALL, n_blocks,
                         FF_CHUNK, v2_s, s * seg_len)
            fhist[c, s] = fbuf[s, seg_len:seg_len + HIST_SMALL, :]
        hid = jax.nn.gelu(v2_s[...]) * g
        acc_s[...] += _bdot(hid.astype(bf16), w_down[c])
        return carry

    lax.fori_loop(0, N_FF, ffn_chunk, 0)
    y_ref[0] = _layer_norm(ALPHA * x1_s[...] + acc_s[...], ln2_g[...], ln2_b[...])

    @pl.when(t == n_t - 1)
    def _():
        for s in range(nseg):
            o_cdw[s] = cbuf[s, HIST_DW - (CONV_W - 1):HIST_DW, :]
            o_lc[s] = lbuf[s, HIST_SMALL - (LRU_CONV_W - 1):HIST_SMALL, :]
            o_h[s] = hst[s, 0:1, :]
            for c in range(N_FF):
                o_fc[s, :, c * FF_CHUNK:(c + 1) * FF_CHUNK] = fhist[c, s, HIST_SMALL - (FFN_CONV_W - 1):HIST_SMALL, :]


def _const_spec(shape):
    zeros = (0,) * len(shape)
    return pl.BlockSpec(shape, lambda g, t: zeros, pipeline_mode=pl.Buffered(1))


def _run_layer(x, states, wts, ln0, nseg, seg_len, start_pos, apply_ln0, name):
    n_groups, total_rows, _ = x.shape
    tile = nseg * seg_len
    n_t = total_rows // tile
    st_cdw, st_lc, st_h, st_fc = states
    f32 = jnp.float32

    def state_spec(arr):
        blk = (nseg,) + arr.shape[1:]
        return pl.BlockSpec(blk, lambda g, t: (g, 0, 0))

    x_spec = pl.BlockSpec((1, tile, D_MODEL), lambda g, t: (g, t, 0))
    operands = [x, st_cdw, st_lc, st_h, st_fc, ln0[0], ln0[1]] + list(wts)
    in_specs = [x_spec] + [state_spec(a) for a in (st_cdw, st_lc, st_h, st_fc)]
    in_specs += [_const_spec(a.shape) for a in operands[5:]]
    out_shape = [jax.ShapeDtypeStruct(x.shape, f32)] + [jax.ShapeDtypeStruct(a.shape, f32) for a in states]
    out_specs = [x_spec] + [state_spec(a) for a in states]
    scratch = [
        pltpu.VMEM((nseg, HIST_DW + seg_len + SUBLANES, D_CONV), f32),
        pltpu.VMEM((nseg, HIST_SMALL + seg_len + SUBLANES, D_LRU), f32),
        pltpu.VMEM((nseg, HIST_SMALL + seg_len + SUBLANES, FF_CHUNK), f32),
        pltpu.VMEM((N_FF, nseg, HIST_SMALL, FF_CHUNK), f32),
        pltpu.VMEM((nseg, SUBLANES, D_LRU), f32),
        pltpu.VMEM((tile, D_MODEL), f32),
        pltpu.VMEM((tile, D_MODEL), jnp.bfloat16),
        pltpu.VMEM((tile, D_CONV), f32),
        pltpu.VMEM((tile, D_LRU), f32),
        pltpu.VMEM((tile, D_LRU), f32),
        pltpu.VMEM((tile, D_LRU), f32),
        pltpu.VMEM((tile, D_MODEL), f32),
        pltpu.VMEM((tile, D_MODEL), f32),
        pltpu.VMEM((tile, D_MODEL), jnp.bfloat16),
        pltpu.VMEM((tile, FF_CHUNK), f32),
        pltpu.VMEM((tile, D_MODEL), f32),
    ]
    return pl.pallas_call(
        functools.partial(_layer_kernel, nseg, seg_len, start_pos, apply_ln0),
        grid=(n_groups, n_t),
        in_specs=in_specs,
        out_specs=out_specs,
        out_shape=out_shape,
        scratch_shapes=scratch,
        compiler_params=pltpu.CompilerParams(
            dimension_semantics=("arbitrary", "arbitrary"),
            vmem_limit_bytes=VMEM_LIMIT_BYTES),
        name=name,
    )(*operands)


def _block_diag_gates(w):
    hd = D_LRU // LRU_HEADS
    per = MXU_DIM // hd
    w4 = w.reshape(D_LRU // MXU_DIM, per, hd, hd)
    eye = jnp.eye(per, dtype=w.dtype)
    return jnp.einsum('xhde,hg->xhdge', w4, eye).reshape(D_LRU // MXU_DIM, MXU_DIM, MXU_DIM).astype(jnp.bfloat16)


def _rep8(w):
    return jnp.broadcast_to(w[:, None, :], (w.shape[0], SUBLANES, w.shape[1]))


def _layer_weights(l, w_in, b_in, conv_dw_w, conv_dw_b, conv_gn_g, conv_gn_b, lru_conv_w, lru_conv_b,
                   lru_wa, lru_ba, lru_wx, lru_bx, lru_lambda, w_out, b_out, ln1_g, ln1_b,
                   ffn_w_up, ffn_conv_w, ffn_conv_b, ffn_w_down, ln2_g, ln2_b):
    bf16 = jnp.bfloat16
    row = lambda v: v.reshape(1, -1)
    grp = D_CONV // CONV_GROUPS
    gidx = jnp.arange(MXU_DIM) // grp
    gmat = jnp.where(gidx[:, None] == gidx[None, :], 1.0 / grp, 0.0).astype(bf16)
    up = ffn_w_up[l].astype(bf16).reshape(D_MODEL, 2 * N_FF, FF_CHUNK).transpose(1, 0, 2)
    fw = _rep8(ffn_conv_w[l]).reshape(FFN_CONV_W, SUBLANES, N_FF, FF_CHUNK).transpose(2, 0, 1, 3)
    return (
        w_in[l].astype(bf16), row(b_in[l]),
        _rep8(conv_dw_w[l]), row(conv_dw_b[l]), row(conv_gn_g[l]), row(conv_gn_b[l]), gmat,
        _rep8(lru_conv_w[l]), row(lru_conv_b[l]),
        _block_diag_gates(lru_wa[l]), _block_diag_gates(lru_wx[l]),
        row(lru_ba[l]), row(lru_bx[l]), row(lru_lambda[l]),
        w_out[l].astype(bf16), row(b_out[l]), row(ln1_g[l]), row(ln1_b[l]),
        up, fw, ffn_conv_b[l].reshape(N_FF, 1, FF_CHUNK),
        ffn_w_down[l].astype(bf16).reshape(N_FF, FF_CHUNK, D_MODEL), row(ln2_g[l]), row(ln2_b[l]),
    )


def kernel(x_prompt, x_sample, state_conv_dw, state_lru_conv, state_lru_h, state_ffn_conv, ln0_g, ln0_b, w_in, b_in, conv_dw_w, conv_dw_b, conv_gn_g, conv_gn_b, lru_conv_w, lru_conv_b, lru_wa, lru_ba, lru_wx, lru_bx, lru_lambda, w_out, b_out, ln1_g, ln1_b, ffn_w_up, ffn_conv_w, ffn_conv_b, ffn_w_down, ln2_g, ln2_b):
    f32 = jnp.float32
    nb, seq, _ = x_prompt.shape
    db, dseq, _ = x_sample.shape
    ln0 = (ln0_g.reshape(1, -1), ln0_b.reshape(1, -1))
    zero_states = (jnp.zeros((nb, CONV_W - 1, D_CONV), f32), jnp.zeros((nb, LRU_CONV_W - 1, D_LRU), f32),
                   jnp.zeros((nb, 1, D_LRU), f32), jnp.zeros((nb, FFN_CONV_W - 1, D_FF), f32))
    xp = x_prompt
    xs = x_sample.reshape(1, db * dseq, D_MODEL)
    p_states, s_states = [], []
    for l in range(DEPTH):
        wts = _layer_weights(l, w_in, b_in, conv_dw_w, conv_dw_b, conv_gn_g, conv_gn_b, lru_conv_w, lru_conv_b,
                             lru_wa, lru_ba, lru_wx, lru_bx, lru_lambda, w_out, b_out, ln1_g, ln1_b,
                             ffn_w_up, ffn_conv_w, ffn_conv_b, ffn_w_down, ln2_g, ln2_b)
        xp, *pst = _run_layer(xp, zero_states, wts, ln0, 1, PROMPT_TILE, 0, l == 0, f"prompt_l{l}")
        p_states.append(pst)
        sst_in = (state_conv_dw[l], state_lru_conv[l], state_lru_h[l].reshape(db, 1, D_LRU), state_ffn_conv[l])
        xs, *sst = _run_layer(xs, sst_in, wts, ln0, db, dseq, PAST_LEN, l == 0, f"sample_l{l}")
        s_states.append(sst)

    def stack(states, k):
        return jnp.stack([st[k] for st in states])

    return (xp, xs.reshape(db, dseq, D_MODEL),
            stack(p_states, 0), stack(p_states, 1), stack(p_states, 2).reshape(DEPTH, nb, D_LRU), stack(p_states, 3),
            stack(s_states, 0), stack(s_states, 1), stack(s_states, 2).reshape(DEPTH, db, D_LRU), stack(s_states, 3))
```

```python
import functools

import jax
import jax.numpy as jnp
from jax import lax
from jax.experimental import pallas as pl
from jax.experimental.pallas import tpu as pltpu

D_MODEL = 1024
D_CONV = 512
D_LRU = 512
CONV_W = 31
LRU_CONV_W = 4
FFN_CONV_W = 3
CONV_GROUPS = 8
LRU_HEADS = 8
LRU_C = 8.0
D_FF = 2816
DEPTH = 4
PAST_LEN = 4096
ALPHA = (2 * DEPTH) ** 0.25
LN_EPS = 1e-5

SUBLANES = 8
LANES = 128
MXU_DIM = 256
VMEM_LIMIT_BYTES = 60000 * 1024

FF_CHUNK = MXU_DIM
N_FF = D_FF // FF_CHUNK
HIST_DW = 32
HIST_SMALL = SUBLANES
PROMPT_TILE = 512


def _rows(j, n=SUBLANES):
    if isinstance(j, int):
        return pl.ds(j * n, n)
    return pl.ds(pl.multiple_of(j * n, n), n)


def _loop(n, body, init, unroll=1):
    if n <= 4:
        carry = init
        for j in range(n):
            carry = body(j, carry)
        return carry
    return lax.fori_loop(0, n, body, init, unroll=unroll)


def _layer_norm(x, g, b):
    mu = jnp.mean(x, axis=-1, keepdims=True)
    d = x - mu
    var = jnp.mean(d * d, axis=-1, keepdims=True)
    return d * lax.rsqrt(var + LN_EPS) * g + b


def _bdot(a, w):
    return jnp.dot(a, w, preferred_element_type=jnp.float32)


def _split_bf16(v):
    hi = v.astype(jnp.bfloat16)
    lo = (v - hi.astype(jnp.float32)).astype(jnp.bfloat16)
    return hi, lo


def _group_mean(v, gmat):
    hi, lo = _split_bf16(v)
    halves = []
    for h in range(D_CONV // MXU_DIM):
        cs = slice(h * MXU_DIM, (h + 1) * MXU_DIM)
        halves.append(_bdot(hi[:, cs], gmat) + _bdot(lo[:, cs], gmat))
    return jnp.concatenate(halves, axis=1)


def _causal_conv(buf, seg, get_w, bias, width, hist, n_blocks, n_ch, out_ref, out_row0):
    base = hist - (width - 1)
    taps = {}
    for k in range(width):
        o = base + k
        taps.setdefault(o % SUBLANES, []).append((o // SUBLANES, k))
    rs = sorted(taps)
    sub = lax.broadcasted_iota(jnp.int32, (SUBLANES, LANES), 0)

    for c0 in range(0, n_ch, LANES):
        cs = slice(c0, c0 + LANES)

        def partials(jb):
            rows = {}
            out = []
            for r in rs:
                acc = None
                for (q, k) in taps[r]:
                    if q not in rows:
                        rows[q] = buf[seg, _rows(jb + q), cs]
                    term = get_w(k, cs) * rows[q]
                    acc = term if acc is None else acc + term
                out.append(acc)
            return tuple(out)

        def body(j, p_cur):
            p_next = partials(j + 1)
            y = bias[:, cs]
            for idx, r in enumerate(rs):
                if r == 0:
                    y = y + p_cur[idx]
                else:
                    sel = jnp.where(sub >= r, p_cur[idx], p_next[idx])
                    y = y + pltpu.roll(sel, SUBLANES - r, 0)
            if isinstance(j, int):
                out_ref[pl.ds(out_row0 + j * SUBLANES, SUBLANES), cs] = y
            else:
                out_ref[pl.ds(pl.multiple_of(out_row0 + j * SUBLANES, SUBLANES), SUBLANES), cs] = y
            return p_next

        _loop(n_blocks, body, partials(0), unroll=2)


def _small_conv(buf, seg, w_ref, bias, width, hist, seg_len, cs):
    base = hist - (width - 1)
    y = bias
    for k in range(width):
        y = y + w_ref[k:k + 1, cs] * buf[seg, base + k:base + k + seg_len, cs]
    return y


def _lru_scan(a_ref, b_ref, h_bcast, row0, n_blocks):
    sub = lax.broadcasted_iota(jnp.int32, (SUBLANES, D_LRU), 0)

    def body(j, h_prev):
        if isinstance(j, int):
            rows = pl.ds(row0 + j * SUBLANES, SUBLANES)
        else:
            rows = pl.ds(pl.multiple_of(row0 + j * SUBLANES, SUBLANES), SUBLANES)
        a = a_ref[rows, :]
        b = b_ref[rows, :]
        for sft in (1, 2, 4):
            a_sh = jnp.where(sub >= sft, pltpu.roll(a, sft, 0), 1.0)
            b_sh = jnp.where(sub >= sft, pltpu.roll(b, sft, 0), 0.0)
            b = a * b_sh + b
            a = a * a_sh
        h = b + a * h_prev
        b_ref[rows, :] = h
        return jnp.broadcast_to(h[SUBLANES - 1:SUBLANES, :], (SUBLANES, D_LRU))

    return _loop(n_blocks, body, h_bcast)


def _layer_kernel(
        nseg, seg_len, start_pos, apply_ln0,
        x_ref, st_cdw, st_lc, st_h, st_fc, ln0_g, ln0_b,
        w_in, b_in, cw, cb, gn_g, gn_b, gmat,
        lw, lb, wa, wx, ba, bx, lam,
        w_out, b_out, ln1_g, ln1_b,
        w_up, fw, fb, w_down, ln2_g, ln2_b,
        y_ref, o_cdw, o_lc, o_h, o_fc,
        cbuf, lbuf, ubuf, hst, xs, xb_s, c_s, a_s, b_s, mixin_s, x1_s, x1b_s, hid_s):
    t = pl.program_id(1)
    n_t = pl.num_programs(1)
    n_blocks = seg_len // SUBLANES
    f32 = jnp.float32
    bf16 = jnp.bfloat16
    segs = [slice(s * seg_len, (s + 1) * seg_len) for s in range(nseg)]

    @pl.when(t == 0)
    def _():
        for s in range(nseg):
            cbuf[s, 0:HIST_DW - (CONV_W - 1), :] = jnp.zeros((HIST_DW - (CONV_W - 1), D_CONV), f32)
            cbuf[s, HIST_DW - (CONV_W - 1):HIST_DW, :] = st_cdw[s]
            cbuf[s, HIST_DW + seg_len:HIST_DW + seg_len + SUBLANES, :] = jnp.zeros((SUBLANES, D_CONV), f32)
            lbuf[s, 0:HIST_SMALL - (LRU_CONV_W - 1), :] = jnp.zeros((HIST_SMALL - (LRU_CONV_W - 1), D_LRU), f32)
            lbuf[s, HIST_SMALL - (LRU_CONV_W - 1):HIST_SMALL, :] = st_lc[s]
            ubuf[s, 0:HIST_SMALL - (FFN_CONV_W - 1), :] = jnp.zeros((HIST_SMALL - (FFN_CONV_W - 1), D_FF), f32)
            ubuf[s, HIST_SMALL - (FFN_CONV_W - 1):HIST_SMALL, :] = st_fc[s]
            hst[s] = jnp.broadcast_to(st_h[s], (SUBLANES, D_LRU))

    x = x_ref[0]
    if apply_ln0:
        x = _layer_norm(x, ln0_g[...], ln0_b[...])
    xs[...] = x
    xb_s[...] = x.astype(bf16)

    za = _bdot(xb_s[...], w_in[:, 0:D_CONV]) + b_in[:, 0:D_CONV]
    zg = _bdot(xb_s[...], w_in[:, D_CONV:2 * D_CONV]) + b_in[:, D_CONV:2 * D_CONV]
    glu = za * jax.nn.sigmoid(zg)
    for s in range(nseg):
        cbuf[s, HIST_DW:HIST_DW + seg_len, :] = glu[segs[s]]
    cb_b = jnp.broadcast_to(cb[...], (SUBLANES, D_CONV))
    for s in range(nseg):
        _causal_conv(cbuf, s, lambda k, cs: cw[k, :, cs], cb_b, CONV_W, HIST_DW, n_blocks, D_CONV,
                     c_s, s * seg_len)
        cbuf[s, 0:HIST_DW, :] = cbuf[s, seg_len:seg_len + HIST_DW, :]
    c = c_s[...]
    d = c - _group_mean(c, gmat[...])
    var = _group_mean(d * d, gmat[...])
    cn = d * lax.rsqrt(var + LN_EPS) * gn_g[...] + gn_b[...]
    mixin_s[:, 0:D_CONV] = (cn * jax.nn.sigmoid(cn)).astype(bf16)

    zx = _bdot(xb_s[...], w_in[:, 2 * D_CONV:2 * D_CONV + D_LRU]) + b_in[:, 2 * D_CONV:2 * D_CONV + D_LRU]
    for s in range(nseg):
        lbuf[s, HIST_SMALL:HIST_SMALL + seg_len, :] = zx[segs[s]]
    xl = jnp.concatenate(
        [_small_conv(lbuf, s, lw, lb[...], LRU_CONV_W, HIST_SMALL, seg_len, slice(None)) for s in range(nseg)], axis=0)
    for s in range(nseg):
        lbuf[s, 0:HIST_SMALL, :] = lbuf[s, seg_len:seg_len + HIST_SMALL, :]
    xlb = xl.astype(bf16)
    ra, ix = [], []
    for h in range(D_LRU // MXU_DIM):
        cs = slice(h * MXU_DIM, (h + 1) * MXU_DIM)
        ra.append(_bdot(xlb[:, cs], wa[h]))
        ix.append(_bdot(xlb[:, cs], wx[h]))
    r = jax.nn.sigmoid(jnp.concatenate(ra, axis=1) + ba[...])
    i = jax.nn.sigmoid(jnp.concatenate(ix, axis=1) + bx[...])
    log_a = r * (-LRU_C * jax.nn.softplus(-lam[...]))
    a = jnp.exp(log_a)
    mult = jnp.sqrt(-jnp.tanh(log_a) * (a * a + 1.0))
    if start_pos == 0:
        row = lax.broadcasted_iota(jnp.int32, (nseg * seg_len, D_LRU), 0)
        if nseg > 1:
            row = row % seg_len
        reset = (row + t * seg_len) == 0
        mult = jnp.where(reset, 1.0, mult)
        a = jnp.where(reset, 0.0, a)
    a_s[...] = a
    b_s[...] = mult * (i * xl)
    for s in range(nseg):
        hst[s] = _lru_scan(a_s, b_s, hst[s], s * seg_len, n_blocks)
    zgate = _bdot(xb_s[...], w_in[:, 2 * D_CONV + D_LRU:]) + b_in[:, 2 * D_CONV + D_LRU:]
    mixin_s[:, D_CONV:] = (jax.nn.gelu(zgate) * b_s[...]).astype(bf16)
    mix = _bdot(mixin_s[...], w_out[...]) + b_out[...]
    x1 = _layer_norm(ALPHA * xs[...] + mix, ln1_g[...], ln1_b[...])
    x1_s[...] = x1
    x1b_s[...] = x1.astype(bf16)

    for c in range(N_FF):
        cs = slice(c * FF_CHUNK, (c + 1) * FF_CHUNK)
        v = _bdot(x1b_s[...], w_up[:, cs])
        for s in range(nseg):
            ubuf[s, HIST_SMALL:HIST_SMALL + seg_len, cs] = v[segs[s]]
        g = _bdot(x1b_s[...], w_up[:, D_FF + c * FF_CHUNK:D_FF + (c + 1) * FF_CHUNK])
        vc = jnp.concatenate(
            [_small_conv(ubuf, s, fw, fb[:, cs], FFN_CONV_W, HIST_SMALL, seg_len, cs) for s in range(nseg)], axis=0)
        hid_s[:, cs] = (jax.nn.gelu(vc) * g).astype(bf16)
    for s in range(nseg):
        ubuf[s, 0:HIST_SMALL, :] = ubuf[s, seg_len:seg_len + HIST_SMALL, :]
    ffn = _bdot(hid_s[...], w_down[...])
    y_ref[0] = _layer_norm(ALPHA * x1_s[...] + ffn, ln2_g[...], ln2_b[...])

    @pl.when(t == n_t - 1)
    def _():
        for s in range(nseg):
            o_cdw[s] = cbuf[s, HIST_DW - (CONV_W - 1):HIST_DW, :]
            o_lc[s] = lbuf[s, HIST_SMALL - (LRU_CONV_W - 1):HIST_SMALL, :]
            o_h[s] = hst[s, 0:1, :]
            o_fc[s] = ubuf[s, HIST_SMALL - (FFN_CONV_W - 1):HIST_SMALL, :]


def _const_spec(shape):
    zeros = (0,) * len(shape)
    return pl.BlockSpec(shape, lambda g, t: zeros, pipeline_mode=pl.Buffered(1))


def _run_layer(x, states, wts, ln0, nseg, seg_len, start_pos, apply_ln0, name):
    n_groups, total_rows, _ = x.shape
    tile = nseg * seg_len
    n_t = total_rows // tile
    st_cdw, st_lc, st_h, st_fc = states
    f32 = jnp.float32
    bf16 = jnp.bfloat16

    def state_spec(arr):
        blk = (nseg,) + arr.shape[1:]
        return pl.BlockSpec(blk, lambda g, t: (g, 0, 0))

    x_spec = pl.BlockSpec((1, tile, D_MODEL), lambda g, t: (g, t, 0))
    operands = [x, st_cdw, st_lc, st_h, st_fc, ln0[0], ln0[1]] + list(wts)
    in_specs = [x_spec] + [state_spec(a) for a in (st_cdw, st_lc, st_h, st_fc)]
    in_specs += [_const_spec(a.shape) for a in operands[5:]]
    out_shape = [jax.ShapeDtypeStruct(x.shape, f32)] + [jax.ShapeDtypeStruct(a.shape, f32) for a in states]
    out_specs = [x_spec] + [state_spec(a) for a in states]
    scratch = [
        pltpu.VMEM((nseg, HIST_DW + seg_len + SUBLANES, D_CONV), f32),
        pltpu.VMEM((nseg, HIST_SMALL + seg_len, D_LRU), f32),
        pltpu.VMEM((nseg, HIST_SMALL + seg_len, D_FF), f32),
        pltpu.VMEM((nseg, SUBLANES, D_LRU), f32),
        pltpu.VMEM((tile, D_MODEL), f32),
        pltpu.VMEM((tile, D_MODEL), bf16),
        pltpu.VMEM((tile, D_CONV), f32),
        pltpu.VMEM((tile, D_LRU), f32),
        pltpu.VMEM((tile, D_LRU), f32),
        pltpu.VMEM((tile, D_MODEL), bf16),
        pltpu.VMEM((tile, D_MODEL), f32),
        pltpu.VMEM((tile, D_MODEL), bf16),
        pltpu.VMEM((tile, D_FF), bf16),
    ]
    return pl.pallas_call(
        functools.partial(_layer_kernel, nseg, seg_len, start_pos, apply_ln0),
        grid=(n_groups, n_t),
        in_specs=in_specs,
        out_specs=out_specs,
        out_shape=out_shape,
        scratch_shapes=scratch,
        compiler_params=pltpu.CompilerParams(
            dimension_semantics=("arbitrary", "arbitrary"),
            vmem_limit_bytes=VMEM_LIMIT_BYTES),
        name=name,
    )(*operands)


def _block_diag_gates(w):
    hd = D_LRU // LRU_HEADS
    per = MXU_DIM // hd
    w4 = w.reshape(D_LRU // MXU_DIM, per, hd, hd)
    eye = jnp.eye(per, dtype=w.dtype)
    return jnp.einsum('xhde,hg->xhdge', w4, eye).reshape(D_LRU // MXU_DIM, MXU_DIM, MXU_DIM).astype(jnp.bfloat16)


def _rep8(w):
    return jnp.broadcast_to(w[:, None, :], (w.shape[0], SUBLANES, w.shape[1]))


def _layer_weights(l, w_in, b_in, conv_dw_w, conv_dw_b, conv_gn_g, conv_gn_b, lru_conv_w, lru_conv_b,
                   lru_wa, lru_ba, lru_wx, lru_bx, lru_lambda, w_out, b_out, ln1_g, ln1_b,
                   ffn_w_up, ffn_conv_w, ffn_conv_b, ffn_w_down, ln2_g, ln2_b):
    bf16 = jnp.bfloat16
    row = lambda v: v.reshape(1, -1)
    grp = D_CONV // CONV_GROUPS
    gidx = jnp.arange(MXU_DIM) // grp
    gmat = jnp.where(gidx[:, None] == gidx[None, :], 1.0 / grp, 0.0).astype(bf16)
    return (
        w_in[l].astype(bf16), row(b_in[l]),
        _rep8(conv_dw_w[l]), row(conv_dw_b[l]), row(conv_gn_g[l]), row(conv_gn_b[l]), gmat,
        lru_conv_w[l], row(lru_conv_b[l]),
        _block_diag_gates(lru_wa[l]), _block_diag_gates(lru_wx[l]),
        row(lru_ba[l]), row(lru_bx[l]), row(lru_lambda[l]),
        w_out[l].astype(bf16), row(b_out[l]), row(ln1_g[l]), row(ln1_b[l]),
        ffn_w_up[l].astype(bf16), ffn_conv_w[l], row(ffn_conv_b[l]),
        ffn_w_down[l].astype(bf16), row(ln2_g[l]), row(ln2_b[l]),
    )


def kernel(x_prompt, x_sample, state_conv_dw, state_lru_conv, state_lru_h, state_ffn_conv, ln0_g, ln0_b, w_in, b_in, conv_dw_w, conv_dw_b, conv_gn_g, conv_gn_b, lru_conv_w, lru_conv_b, lru_wa, lru_ba, lru_wx, lru_bx, lru_lambda, w_out, b_out, ln1_g, ln1_b, ffn_w_up, ffn_conv_w, ffn_conv_b, ffn_w_down, ln2_g, ln2_b):
    f32 = jnp.float32
    nb, seq, _ = x_prompt.shape
    db, dseq, _ = x_sample.shape
    ln0 = (ln0_g.reshape(1, -1), ln0_b.reshape(1, -1))
    zero_states = (jnp.zeros((nb, CONV_W - 1, D_CONV), f32), jnp.zeros((nb, LRU_CONV_W - 1, D_LRU), f32),
                   jnp.zeros((nb, 1, D_LRU), f32), jnp.zeros((nb, FFN_CONV_W - 1, D_FF), f32))
    xp = x_prompt
    xs = x_sample.reshape(1, db * dseq, D_MODEL)
    p_states, s_states = [], []
    for l in range(DEPTH):
        wts = _layer_weights(l, w_in, b_in, conv_dw_w, conv_dw_b, conv_gn_g, conv_gn_b, lru_conv_w, lru_conv_b,
                             lru_wa, lru_ba, lru_wx, lru_bx, lru_lambda, w_out, b_out, ln1_g, ln1_b,
                             ffn_w_up, ffn_conv_w, ffn_conv_b, ffn_w_down, ln2_g, ln2_b)
        xp, *pst = _run_layer(xp, zero_states, wts, ln0, 1, PROMPT_TILE, 0, l == 0, f"prompt_l{l}")
        p_states.append(pst)
        sst_in = (state_conv_dw[l], state_lru_conv[l], state_lru_h[l].reshape(db, 1, D_LRU), state_ffn_conv[l])
        xs, *sst = _run_layer(xs, sst_in, wts, ln0, db, dseq, PAST_LEN, l == 0, f"sample_l{l}")
        s_states.append(sst)

    def stack(states, k):
        return jnp.stack([st[k] for st in states])

    return (xp, xs.reshape(db, dseq, D_MODEL),
            stack(p_states, 0), stack(p_states, 1), stack(p_states, 2).reshape(DEPTH, nb, D_LRU), stack(p_states, 3),
            stack(s_states, 0), stack(s_states, 1), stack(s_states, 2).reshape(DEPTH, db, D_LRU), stack(s_states, 3))
```

```python
import functools

import jax
import jax.numpy as jnp
from jax import lax
from jax.experimental import pallas as pl
from jax.experimental.pallas import tpu as pltpu

D_MODEL = 1024
D_CONV = 512
D_LRU = 512
CONV_W = 31
LRU_CONV_W = 4
FFN_CONV_W = 3
CONV_GROUPS = 8
LRU_HEADS = 8
LRU_C = 8.0
D_FF = 2816
DEPTH = 4
PAST_LEN = 4096
ALPHA = (2 * DEPTH) ** 0.25
LN_EPS = 1e-5

SUBLANES = 8
LANES = 128
MXU_DIM = 256
VMEM_LIMIT_BYTES = 60000 * 1024

FF_CHUNK = MXU_DIM
N_FF = D_FF // FF_CHUNK
PROMPT_NS = SUBLANES
PROMPT_NI = 64


def _layer_norm(x, g, b):
    mu = jnp.mean(x, axis=-1, keepdims=True)
    d = x - mu
    var = jnp.mean(d * d, axis=-1, keepdims=True)
    return d * lax.rsqrt(var + LN_EPS) * g + b


def _sigmoid(x):
    return 0.5 * jnp.tanh(0.5 * x) + 0.5


def _bdot(a, w):
    return jnp.dot(a, w, preferred_element_type=jnp.float32)


def _split_bf16(v):
    hi = v.astype(jnp.bfloat16)
    lo = (v - hi.astype(jnp.float32)).astype(jnp.bfloat16)
    return hi, lo


def _group_mean(v, gmat):
    hi, lo = _split_bf16(v)
    halves = []
    for h in range(D_CONV // MXU_DIM):
        cs = slice(h * MXU_DIM, (h + 1) * MXU_DIM)
        halves.append(_bdot(hi[:, cs], gmat) + _bdot(lo[:, cs], gmat))
    return jnp.concatenate(halves, axis=1)


def _sublane_iota(n_ch):
    return lax.broadcasted_iota(jnp.int32, (SUBLANES, n_ch), 0)


def _load_hist(hist, st, n_hist, n_ch, ns, chained):
    if chained:
        hist[...] = jnp.zeros(hist.shape, jnp.float32)
    for j in range(n_hist):
        rows = st[0, :, j * n_ch:(j + 1) * n_ch]
        if chained:
            hist[(j + 1) * ns - 1:(j + 1) * ns, :] = rows
        else:
            hist[j * ns:(j + 1) * ns, :] = rows


def _store_hist(out, hist, n_hist, n_ch, ns, chained):
    for j in range(n_hist):
        if chained:
            out[0, :, j * n_ch:(j + 1) * n_ch] = hist[(j + 1) * ns - 1:(j + 1) * ns, :]
        else:
            out[0, :, j * n_ch:(j + 1) * n_ch] = hist[j * ns:(j + 1) * ns, :]


def _fill_head(buf, hist, n_hist, ni, ns, chained, cs):
    for j in range(n_hist):
        old = hist[j * ns:(j + 1) * ns, cs]
        if chained:
            cur = buf[(ni + j) * ns:(ni + j + 1) * ns, cs]
            sub = _sublane_iota(old.shape[1])
            old = pltpu.roll(jnp.where(sub == ns - 1, old, cur), 1, 0)
        buf[j * ns:(j + 1) * ns, cs] = old


def _conv_tile(buf, w_ref, bias, width, n_rows, ns, cs):
    y = bias
    for k in range(width):
        y = y + w_ref[k:k + 1, cs] * buf[k * ns:k * ns + n_rows, cs]
    return y


def _conv_blocks(buf, w8_ref, bias8, width, ni, ns, n_ch, out_ref):
    for i in range(ni):
        for h in range(ns // SUBLANES):
            for c0 in range(0, n_ch, LANES):
                cs = slice(c0, c0 + LANES)
                acc = [bias8[:, cs], None]
                for k in range(width):
                    r0 = (i + k) * ns + h * SUBLANES
                    term = w8_ref[k, :, cs] * buf[r0:r0 + SUBLANES, cs]
                    acc[k % 2] = term if acc[k % 2] is None else acc[k % 2] + term
                out_ref[i * ns + h * SUBLANES:i * ns + (h + 1) * SUBLANES, cs] = acc[0] + acc[1]


def _lru_scan(a, b, hst, ni, ns, chained):
    blk = lambda v, i: v[i * ns:(i + 1) * ns]
    hs, cum = [], []
    if chained:
        h, acum = blk(b, 0), blk(a, 0)
    else:
        h = blk(a, 0) * hst[...] + blk(b, 0)
    hs.append(h)
    if chained:
        cum.append(acum)
    for i in range(1, ni):
        h = blk(a, i) * h + blk(b, i)
        hs.append(h)
        if chained:
            acum = blk(a, i) * acum
            cum.append(acum)
    if not chained:
        hst[...] = hs[-1]
        return jnp.concatenate(hs, axis=0)
    sub = _sublane_iota(D_LRU)
    a_inc, b_inc = cum[-1], hs[-1]
    for sft in (1, 2, 4):
        a_sh = jnp.where(sub >= sft, pltpu.roll(a_inc, sft, 0), 1.0)
        b_sh = jnp.where(sub >= sft, pltpu.roll(b_inc, sft, 0), 0.0)
        b_inc = a_inc * b_sh + b_inc
        a_inc = a_inc * a_sh
    h_prev = hst[...]
    seg_end = b_inc + a_inc * h_prev
    carry = jnp.where(sub == 0, h_prev, pltpu.roll(seg_end, 1, 0))
    hst[...] = jnp.broadcast_to(seg_end[ns - 1:ns, :], (ns, D_LRU))
    return jnp.concatenate([hl + ac * carry for hl, ac in zip(hs, cum)], axis=0)


def _layer_kernel(
        ns, ni, chained, start_pos, apply_ln0,
        x_ref, st_cdw, st_lc, st_h, st_fc, ln0_g, ln0_b,
        w_in, b_in, cw, cb, gn_g, gn_b, gmat,
        lw, lb, wa, wx, ba, bx, lam,
        w_out, b_out, ln1_g, ln1_b,
        w_up, fw, fb, w_down, ln2_g, ln2_b,
        y_ref, o_cdw, o_lc, o_h, o_fc,
        hist_dw, hist_lc, hist_fc, hst, cbuf, lbuf, ubuf, xs, xb_s, c_s, mixin_s, x1_s, x1b_s, hid_s):
    t = pl.program_id(1)
    n_t = pl.num_programs(1)
    n_rows = ni * ns
    h_dw, h_lc, h_fc = CONV_W - 1, LRU_CONV_W - 1, FFN_CONV_W - 1
    bf16 = jnp.bfloat16
    if chained:
        assert ns == SUBLANES and ni >= h_dw

    @pl.when(t == 0)
    def _():
        _load_hist(hist_dw, st_cdw, h_dw, D_CONV, ns, chained)
        _load_hist(hist_lc, st_lc, h_lc, D_LRU, ns, chained)
        _load_hist(hist_fc, st_fc, h_fc, D_FF, ns, chained)
        hst[...] = jnp.broadcast_to(st_h[0], (ns, D_LRU))

    x = jnp.concatenate([x_ref[0, :, i * D_MODEL:(i + 1) * D_MODEL] for i in range(ni)], axis=0)
    if apply_ln0:
        x = _layer_norm(x, ln0_g[...], ln0_b[...])
    xs[...] = x
    xb_s[...] = x.astype(bf16)

    za = _bdot(xb_s[...], w_in[:, 0:D_CONV]) + b_in[:, 0:D_CONV]
    zg = _bdot(xb_s[...], w_in[:, D_CONV:2 * D_CONV]) + b_in[:, D_CONV:2 * D_CONV]
    cbuf[h_dw * ns:, :] = za * _sigmoid(zg)
    _fill_head(cbuf, hist_dw, h_dw, ni, ns, chained, slice(None))
    _conv_blocks(cbuf, cw, jnp.broadcast_to(cb[...], (SUBLANES, D_CONV)), CONV_W, ni, ns, D_CONV, c_s)
    hist_dw[...] = cbuf[ni * ns:, :]
    c = c_s[...]
    d = c - _group_mean(c, gmat[...])
    var = _group_mean(d * d, gmat[...])
    cn = d * lax.rsqrt(var + LN_EPS) * gn_g[...] + gn_b[...]
    mixin_s[:, 0:D_CONV] = (cn * _sigmoid(cn)).astype(bf16)

    lbuf[h_lc * ns:, :] = (_bdot(xb_s[...], w_in[:, 2 * D_CONV:2 * D_CONV + D_LRU])
                           + b_in[:, 2 * D_CONV:2 * D_CONV + D_LRU])
    _fill_head(lbuf, hist_lc, h_lc, ni, ns, chained, slice(None))
    xl = _conv_tile(lbuf, lw, lb[...], LRU_CONV_W, n_rows, ns, slice(None))
    hist_lc[...] = lbuf[ni * ns:, :]
    xlb = xl.astype(bf16)
    ra, ix = [], []
    for h in range(D_LRU // MXU_DIM):
        cs = slice(h * MXU_DIM, (h + 1) * MXU_DIM)
        ra.append(_bdot(xlb[:, cs], wa[h]))
        ix.append(_bdot(xlb[:, cs], wx[h]))
    r = _sigmoid(jnp.concatenate(ra, axis=1) + ba[...])
    gate_i = _sigmoid(jnp.concatenate(ix, axis=1) + bx[...])
    log_a = r * (-LRU_C * jax.nn.softplus(-lam[...]))
    a = jnp.exp(log_a)
    mult = jnp.sqrt(-jnp.tanh(log_a) * (a * a + 1.0))
    if start_pos == 0:
        row = lax.broadcasted_iota(jnp.int32, (n_rows, D_LRU), 0)
        first_rows = 1 if chained else ns
        reset = jnp.logical_and(row < first_rows, t == 0)
        mult = jnp.where(reset, 1.0, mult)
        a = jnp.where(reset, 0.0, a)
    hseq = _lru_scan(a, mult * (gate_i * xl), hst, ni, ns, chained)
    zgate = _bdot(xb_s[...], w_in[:, 2 * D_CONV + D_LRU:]) + b_in[:, 2 * D_CONV + D_LRU:]
    mixin_s[:, D_CONV:] = (jax.nn.gelu(zgate) * hseq).astype(bf16)
    mix = _bdot(mixin_s[...], w_out[...]) + b_out[...]
    x1 = _layer_norm(ALPHA * xs[...] + mix, ln1_g[...], ln1_b[...])
    x1_s[...] = x1
    x1b_s[...] = x1.astype(bf16)

    for c in range(N_FF):
        cs = slice(c * FF_CHUNK, (c + 1) * FF_CHUNK)
        ubuf[h_fc * ns:, cs] = _bdot(x1b_s[...], w_up[:, cs])
        _fill_head(ubuf, hist_fc, h_fc, ni, ns, chained, cs)
        g = _bdot(x1b_s[...], w_up[:, D_FF + c * FF_CHUNK:D_FF + (c + 1) * FF_CHUNK])
        vc = _conv_tile(ubuf, fw, fb[:, cs], FFN_CONV_W, n_rows, ns, cs)
        hid_s[:, cs] = (jax.nn.gelu(vc) * g).astype(bf16)
    hist_fc[...] = ubuf[ni * ns:, :]
    ffn = _bdot(hid_s[...], w_down[...])
    y = _layer_norm(ALPHA * x1_s[...] + ffn, ln2_g[...], ln2_b[...])
    for i in range(ni):
        y_ref[0, :, i * D_MODEL:(i + 1) * D_MODEL] = y[i * ns:(i + 1) * ns]

    @pl.when(t == n_t - 1)
    def _():
        _store_hist(o_cdw, hist_dw, h_dw, D_CONV, ns, chained)
        _store_hist(o_lc, hist_lc, h_lc, D_LRU, ns, chained)
        _store_hist(o_fc, hist_fc, h_fc, D_FF, ns, chained)
        o_h[0] = hst[0:o_h.shape[1], :]


def _const_spec(shape):
    zeros = (0,) * len(shape)
    return pl.BlockSpec(shape, lambda g, t: zeros, pipeline_mode=pl.Buffered(1))


def _run_layer(x, states, wts, ln0, ns, ni, chained, start_pos, apply_ln0, name):
    n_groups, n_t = x.shape[0], x.shape[1]
    n_rows = ns * ni
    f32 = jnp.float32
    bf16 = jnp.bfloat16
    x2 = x.reshape(n_groups * n_t, ns, ni * D_MODEL)

    def state_spec(arr):
        return pl.BlockSpec((1,) + arr.shape[1:], lambda g, t: (g, 0, 0))

    x_spec = pl.BlockSpec((1, ns, ni * D_MODEL), lambda g, t: (g * n_t + t, 0, 0))
    operands = [x2] + list(states) + [ln0[0], ln0[1]] + list(wts)
    in_specs = [x_spec] + [state_spec(a) for a in states]
    in_specs += [_const_spec(a.shape) for a in operands[5:]]
    out_shape = [jax.ShapeDtypeStruct(x2.shape, f32)] + [jax.ShapeDtypeStruct(a.shape, f32) for a in states]
    out_specs = [x_spec] + [state_spec(a) for a in states]
    h_dw, h_lc, h_fc = CONV_W - 1, LRU_CONV_W - 1, FFN_CONV_W - 1
    scratch = [
        pltpu.VMEM((h_dw * ns, D_CONV), f32),
        pltpu.VMEM((h_lc * ns, D_LRU), f32),
        pltpu.VMEM((h_fc * ns, D_FF), f32),
        pltpu.VMEM((ns, D_LRU), f32),
        pltpu.VMEM(((h_dw + ni) * ns, D_CONV), f32),
        pltpu.VMEM(((h_lc + ni) * ns, D_LRU), f32),
        pltpu.VMEM(((h_fc + ni) * ns, D_FF), f32),
        pltpu.VMEM((n_rows, D_MODEL), f32),
        pltpu.VMEM((n_rows, D_MODEL), bf16),
        pltpu.VMEM((n_rows, D_CONV), f32),
        pltpu.VMEM((n_rows, D_MODEL), bf16),
        pltpu.VMEM((n_rows, D_MODEL), f32),
        pltpu.VMEM((n_rows, D_MODEL), bf16),
        pltpu.VMEM((n_rows, D_FF), bf16),
    ]
    y, *new_states = pl.pallas_call(
        functools.partial(_layer_kernel, ns, ni, chained, start_pos, apply_ln0),
        grid=(n_groups, n_t),
        in_specs=in_specs,
        out_specs=out_specs,
        out_shape=out_shape,
        scratch_shapes=scratch,
        compiler_params=pltpu.CompilerParams(
            dimension_semantics=("arbitrary", "arbitrary"),
            vmem_limit_bytes=VMEM_LIMIT_BYTES),
        name=name,
    )(*operands)
    return y.reshape(x.shape), new_states


def _block_diag_gates(w):
    hd = D_LRU // LRU_HEADS
    per = MXU_DIM // hd
    w4 = w.reshape(D_LRU // MXU_DIM, per, hd, hd)
    eye = jnp.eye(per, dtype=w.dtype)
    return jnp.einsum('xhde,hg->xhdge', w4, eye).reshape(D_LRU // MXU_DIM, MXU_DIM, MXU_DIM).astype(jnp.bfloat16)


def _rep8(w):
    return jnp.broadcast_to(w[:, None, :], (w.shape[0], SUBLANES, w.shape[1]))


def _layer_weights(l, w_in, b_in, conv_dw_w, conv_dw_b, conv_gn_g, conv_gn_b, lru_conv_w, lru_conv_b,
                   lru_wa, lru_ba, lru_wx, lru_bx, lru_lambda, w_out, b_out, ln1_g, ln1_b,
                   ffn_w_up, ffn_conv_w, ffn_conv_b, ffn_w_down, ln2_g, ln2_b):
    bf16 = jnp.bfloat16
    row = lambda v: v.reshape(1, -1)
    grp = D_CONV // CONV_GROUPS
    gidx = jnp.arange(MXU_DIM) // grp
    gmat = jnp.where(gidx[:, None] == gidx[None, :], 1.0 / grp, 0.0).astype(bf16)
    return (
        w_in[l].astype(bf16), row(b_in[l]),
        _rep8(conv_dw_w[l]), row(conv_dw_b[l]), row(conv_gn_g[l]), row(conv_gn_b[l]), gmat,
        lru_conv_w[l], row(lru_conv_b[l]),
        _block_diag_gates(lru_wa[l]), _block_diag_gates(lru_wx[l]),
        row(lru_ba[l]), row(lru_bx[l]), row(lru_lambda[l]),
        w_out[l].astype(bf16), row(b_out[l]), row(ln1_g[l]), row(ln1_b[l]),
        ffn_w_up[l].astype(bf16), ffn_conv_w[l], row(ffn_conv_b[l]),
        ffn_w_down[l].astype(bf16), row(ln2_g[l]), row(ln2_b[l]),
    )


def kernel(x_prompt, x_sample, state_conv_dw, state_lru_conv, state_lru_h, state_ffn_conv, ln0_g, ln0_b, w_in, b_in, conv_dw_w, conv_dw_b, conv_gn_g, conv_gn_b, lru_conv_w, lru_conv_b, lru_wa, lru_ba, lru_wx, lru_bx, lru_lambda, w_out, b_out, ln1_g, ln1_b, ffn_w_up, ffn_conv_w, ffn_conv_b, ffn_w_down, ln2_g, ln2_b):
    f32 = jnp.float32
    nb, seq, _ = x_prompt.shape
    db, dseq, _ = x_sample.shape
    ln0 = (ln0_g.reshape(1, -1), ln0_b.reshape(1, -1))
    tile = PROMPT_NS * PROMPT_NI
    p_in = (jnp.zeros((nb, 1, (CONV_W - 1) * D_CONV), f32), jnp.zeros((nb, 1, (LRU_CONV_W - 1) * D_LRU), f32),
            jnp.zeros((nb, 1, D_LRU), f32), jnp.zeros((nb, 1, (FFN_CONV_W - 1) * D_FF), f32))
    xp = x_prompt.reshape(nb, seq // tile, PROMPT_NS, PROMPT_NI * D_MODEL)
    xs = x_sample.reshape(1, 1, db, dseq * D_MODEL)
    p_states, s_states = [], []
    for l in range(DEPTH):
        wts = _layer_weights(l, w_in, b_in, conv_dw_w, conv_dw_b, conv_gn_g, conv_gn_b, lru_conv_w, lru_conv_b,
                             lru_wa, lru_ba, lru_wx, lru_bx, lru_lambda, w_out, b_out, ln1_g, ln1_b,
                             ffn_w_up, ffn_conv_w, ffn_conv_b, ffn_w_down, ln2_g, ln2_b)
        xp, pst = _run_layer(xp, p_in, wts, ln0, PROMPT_NS, PROMPT_NI, True, 0, l == 0, f"prompt_l{l}")
        p_states.append(pst)
        s_in = (state_conv_dw[l].reshape(1, db, -1), state_lru_conv[l].reshape(1, db, -1),
                state_lru_h[l].reshape(1, db, D_LRU), state_ffn_conv[l].reshape(1, db, -1))
        xs, sst = _run_layer(xs, s_in, wts, ln0, db, dseq, False, PAST_LEN, l == 0, f"sample_l{l}")
        s_states.append(sst)

    def stack(states, k, shape):
        return jnp.stack([st[k] for st in states]).reshape((DEPTH,) + shape)

    return (xp.reshape(nb, seq, D_MODEL), xs.reshape(db, dseq, D_MODEL),
            stack(p_states, 0, (nb, CONV_W - 1, D_CONV)), stack(p_states, 1, (nb, LRU_CONV_W - 1, D_LRU)),
            stack(p_states, 2, (nb, D_LRU)), stack(p_states, 3, (nb, FFN_CONV_W - 1, D_FF)),
            stack(s_states, 0, (db, CONV_W - 1, D_CONV)), stack(s_states, 1, (db, LRU_CONV_W - 1, D_LRU)),
            stack(s_states, 2, (db, D_LRU)), stack(s_states, 3, (db, FFN_CONV_W - 1, D_FF)))
```

```python
import functools

import jax
import jax.numpy as jnp
from jax import lax
from jax.experimental import pallas as pl
from jax.experimental.pallas import tpu as pltpu

D_MODEL = 1024
D_CONV = 512
D_LRU = 512
CONV_W = 31
LRU_CONV_W = 4
FFN_CONV_W = 3
CONV_GROUPS = 8
LRU_HEADS = 8
LRU_C = 8.0
D_FF = 2816
DEPTH = 4
PAST_LEN = 4096
ALPHA = (2 * DEPTH) ** 0.25
LN_EPS = 1e-5

SUBLANES = 8
LANES = 128
MXU_DIM = 256
VMEM_LIMIT_BYTES = 60000 * 1024

FF_CHUNK = MXU_DIM
N_FF = D_FF // FF_CHUNK
PROMPT_NS = SUBLANES
PROMPT_NI = 64


def _layer_norm(x, g, b):
    mu = jnp.mean(x, axis=-1, keepdims=True)
    d = x - mu
    var = jnp.mean(d * d, axis=-1, keepdims=True)
    return d * lax.rsqrt(var + LN_EPS) * g + b


def _sigmoid(x):
    return 0.5 * jnp.tanh(0.5 * x) + 0.5


def _bdot(a, w):
    return jnp.dot(a, w, preferred_element_type=jnp.float32)


def _split_bf16(v):
    hi = v.astype(jnp.bfloat16)
    lo = (v - hi.astype(jnp.float32)).astype(jnp.bfloat16)
    return hi, lo


def _group_mean(v, gmat):
    hi, lo = _split_bf16(v)
    halves = []
    for h in range(D_CONV // MXU_DIM):
        cs = slice(h * MXU_DIM, (h + 1) * MXU_DIM)
        halves.append(_bdot(hi[:, cs], gmat) + _bdot(lo[:, cs], gmat))
    return jnp.concatenate(halves, axis=1)


def _sublane_iota(n_ch):
    return lax.broadcasted_iota(jnp.int32, (SUBLANES, n_ch), 0)


def _load_hist(hist, st, n_hist, n_ch, ns, chained):
    if chained:
        hist[...] = jnp.zeros(hist.shape, jnp.float32)
    for j in range(n_hist):
        rows = st[0, :, j * n_ch:(j + 1) * n_ch]
        if chained:
            hist[(j + 1) * ns - 1:(j + 1) * ns, :] = rows
        else:
            hist[j * ns:(j + 1) * ns, :] = rows


def _store_hist(out, hist, n_hist, n_ch, ns, chained):
    for j in range(n_hist):
        if chained:
            out[0, :, j * n_ch:(j + 1) * n_ch] = hist[(j + 1) * ns - 1:(j + 1) * ns, :]
        else:
            out[0, :, j * n_ch:(j + 1) * n_ch] = hist[j * ns:(j + 1) * ns, :]


def _fill_head(buf, hist, n_hist, ni, ns, chained, cs):
    for j in range(n_hist):
        old = hist[j * ns:(j + 1) * ns, cs]
        if chained:
            cur = buf[(ni + j) * ns:(ni + j + 1) * ns, cs]
            sub = _sublane_iota(old.shape[1])
            old = pltpu.roll(jnp.where(sub == ns - 1, old, cur), 1, 0)
        buf[j * ns:(j + 1) * ns, cs] = old


def _conv_tile(buf, w_ref, bias, width, n_rows, ns, cs):
    y = bias
    for k in range(width):
        y = y + w_ref[k:k + 1, cs] * buf[k * ns:k * ns + n_rows, cs]
    return y


def _conv_blocks(buf, w8_ref, bias8, width, blocks, ns, n_ch, out_ref):
    for i in blocks:
        for h in range(ns // SUBLANES):
            for c0 in range(0, n_ch, LANES):
                cs = slice(c0, c0 + LANES)
                acc = [bias8[:, cs], None]
                for k in range(width):
                    r0 = (i + k) * ns + h * SUBLANES
                    term = w8_ref[k, :, cs] * buf[r0:r0 + SUBLANES, cs]
                    acc[k % 2] = term if acc[k % 2] is None else acc[k % 2] + term
                out_ref[i * ns + h * SUBLANES:i * ns + (h + 1) * SUBLANES, cs] = acc[0] + acc[1]


def _lru_scan(a, b, hst, ni, ns, chained):
    blk = lambda v, i: v[i * ns:(i + 1) * ns]
    hs, cum = [], []
    if chained:
        h, acum = blk(b, 0), blk(a, 0)
    else:
        h = blk(a, 0) * hst[...] + blk(b, 0)
    hs.append(h)
    if chained:
        cum.append(acum)
    for i in range(1, ni):
        h = blk(a, i) * h + blk(b, i)
        hs.append(h)
        if chained:
            acum = blk(a, i) * acum
            cum.append(acum)
    if not chained:
        hst[...] = hs[-1]
        return jnp.concatenate(hs, axis=0)
    sub = _sublane_iota(D_LRU)
    a_inc, b_inc = cum[-1], hs[-1]
    for sft in (1, 2, 4):
        a_sh = jnp.where(sub >= sft, pltpu.roll(a_inc, sft, 0), 1.0)
        b_sh = jnp.where(sub >= sft, pltpu.roll(b_inc, sft, 0), 0.0)
        b_inc = a_inc * b_sh + b_inc
        a_inc = a_inc * a_sh
    h_prev = hst[...]
    seg_end = b_inc + a_inc * h_prev
    carry = jnp.where(sub == 0, h_prev, pltpu.roll(seg_end, 1, 0))
    hst[...] = jnp.broadcast_to(seg_end[ns - 1:ns, :], (ns, D_LRU))
    return jnp.concatenate([hl + ac * carry for hl, ac in zip(hs, cum)], axis=0)


def _layer_kernel(
        ns, ni, chained, skew, start_pos, apply_ln0,
        x_ref, st_cdw, st_lc, st_h, st_fc, ln0_g, ln0_b,
        w_in, b_in, cw, cb, gn_g, gn_b, gmat,
        lw, lb, wa, wx, ba, bx, lam,
        w_out, b_out, ln1_g, ln1_b,
        w_up, fw, fb, w_down, ln2_g, ln2_b,
        y_ref, o_cdw, o_lc, o_h, o_fc,
        hist_dw, hist_lc, hist_fc, hst, cbuf, lbuf, ubuf, xs, xb_s, c_s, mixin_s, x1_s, x1b_s, hid_s):
    t = pl.program_id(1)
    n_steps = pl.num_programs(1)
    n_rows = ni * ns
    h_dw, h_lc, h_fc = CONV_W - 1, LRU_CONV_W - 1, FFN_CONV_W - 1
    f32 = jnp.float32
    bf16 = jnp.bfloat16
    if chained:
        assert ns == SUBLANES and ni >= h_dw
    mixer_first, ffn_first = 0, (1 if skew else 0)
    mixer_last, ffn_last = (n_steps - 2 if skew else n_steps - 1), n_steps - 1

    @pl.when(t == mixer_first)
    def _():
        _load_hist(hist_dw, st_cdw, h_dw, D_CONV, ns, chained)
        _load_hist(hist_lc, st_lc, h_lc, D_LRU, ns, chained)
        hst[...] = jnp.broadcast_to(st_h[0], (ns, D_LRU))
        if skew:
            hist_fc[...] = jnp.zeros(hist_fc.shape, f32)

            def zero_rows(i, carry):
                rows = pl.ds(pl.multiple_of(i * 2 * SUBLANES, 2 * SUBLANES), 2 * SUBLANES)
                x1_s[rows, :] = jnp.zeros((2 * SUBLANES, D_MODEL), f32)
                x1b_s[rows, :] = jnp.zeros((2 * SUBLANES, D_MODEL), bf16)
                return carry

            lax.fori_loop(0, n_rows // (2 * SUBLANES), zero_rows, 0)

    @pl.when(t == ffn_first)
    def _():
        _load_hist(hist_fc, st_fc, h_fc, D_FF, ns, chained)

    v = {}

    def m_input():
        x = jnp.concatenate([x_ref[0, :, i * D_MODEL:(i + 1) * D_MODEL] for i in range(ni)], axis=0)
        if apply_ln0:
            x = _layer_norm(x, ln0_g[...], ln0_b[...])
        xs[...] = x
        xb_s[...] = x.astype(bf16)

    def m_glu():
        za = _bdot(xb_s[...], w_in[:, 0:D_CONV]) + b_in[:, 0:D_CONV]
        zg = _bdot(xb_s[...], w_in[:, D_CONV:2 * D_CONV]) + b_in[:, D_CONV:2 * D_CONV]
        cbuf[h_dw * ns:, :] = za * _sigmoid(zg)
        _fill_head(cbuf, hist_dw, h_dw, ni, ns, chained, slice(None))

    def m_conv(part):
        blocks = range(part * ni // N_FF, (part + 1) * ni // N_FF)
        _conv_blocks(cbuf, cw, jnp.broadcast_to(cb[...], (SUBLANES, D_CONV)), CONV_W, blocks, ns, D_CONV, c_s)
        if part == N_FF - 1:
            hist_dw[...] = cbuf[ni * ns:, :]

    def m_gn():
        c = c_s[...]
        d = c - _group_mean(c, gmat[...])
        var = _group_mean(d * d, gmat[...])
        cn = d * lax.rsqrt(var + LN_EPS) * gn_g[...] + gn_b[...]
        mixin_s[:, 0:D_CONV] = (cn * _sigmoid(cn)).astype(bf16)

    def m_lconv():
        lbuf[h_lc * ns:, :] = (_bdot(xb_s[...], w_in[:, 2 * D_CONV:2 * D_CONV + D_LRU])
                               + b_in[:, 2 * D_CONV:2 * D_CONV + D_LRU])
        _fill_head(lbuf, hist_lc, h_lc, ni, ns, chained, slice(None))
        v["xl"] = _conv_tile(lbuf, lw, lb[...], LRU_CONV_W, n_rows, ns, slice(None))
        hist_lc[...] = lbuf[ni * ns:, :]

    def m_gates():
        xl = v["xl"]
        xlb = xl.astype(bf16)
        ra, ix = [], []
        for h in range(D_LRU // MXU_DIM):
            cs = slice(h * MXU_DIM, (h + 1) * MXU_DIM)
            ra.append(_bdot(xlb[:, cs], wa[h]))
            ix.append(_bdot(xlb[:, cs], wx[h]))
        r = _sigmoid(jnp.concatenate(ra, axis=1) + ba[...])
        gate_i = _sigmoid(jnp.concatenate(ix, axis=1) + bx[...])
        log_a = r * (-LRU_C * jax.nn.softplus(-lam[...]))
        a = jnp.exp(log_a)
        mult = jnp.sqrt(-jnp.tanh(log_a) * (a * a + 1.0))
        if start_pos == 0:
            row = lax.broadcasted_iota(jnp.int32, (n_rows, D_LRU), 0)
            first_rows = 1 if chained else ns
            reset = jnp.logical_and(row < first_rows, t == 0)
            mult = jnp.where(reset, 1.0, mult)
            a = jnp.where(reset, 0.0, a)
        v["a"], v["b"] = a, mult * (gate_i * xl)

    def m_scan():
        hseq = _lru_scan(v["a"], v["b"], hst, ni, ns, chained)
        zgate = _bdot(xb_s[...], w_in[:, 2 * D_CONV + D_LRU:]) + b_in[:, 2 * D_CONV + D_LRU:]
        mixin_s[:, D_CONV:] = (jax.nn.gelu(zgate) * hseq).astype(bf16)

    def m_out():
        mix = _bdot(mixin_s[...], w_out[...]) + b_out[...]
        x1 = _layer_norm(ALPHA * xs[...] + mix, ln1_g[...], ln1_b[...])
        x1_s[...] = x1
        x1b_s[...] = x1.astype(bf16)

    def f_up(c):
        cs = slice(c * FF_CHUNK, (c + 1) * FF_CHUNK)
        ubuf[h_fc * ns:, cs] = _bdot(x1b_s[...], w_up[:, cs])
        _fill_head(ubuf, hist_fc, h_fc, ni, ns, chained, cs)
        g = _bdot(x1b_s[...], w_up[:, D_FF + c * FF_CHUNK:D_FF + (c + 1) * FF_CHUNK])
        vc = _conv_tile(ubuf, fw, fb[:, cs], FFN_CONV_W, n_rows, ns, cs)
        hid_s[:, cs] = (jax.nn.gelu(vc) * g).astype(bf16)
        if c == N_FF - 1:
            hist_fc[...] = ubuf[ni * ns:, :]

    def f_down(n):
        v["ffn%d" % n] = _bdot(hid_s[...], w_down[:, n * MXU_DIM:(n + 1) * MXU_DIM])

    def f_out():
        ffn = jnp.concatenate([v["ffn%d" % n] for n in range(D_MODEL // MXU_DIM)], axis=1)
        y = _layer_norm(ALPHA * x1_s[...] + ffn, ln2_g[...], ln2_b[...])
        for i in range(ni):
            y_ref[0, :, i * D_MODEL:(i + 1) * D_MODEL] = y[i * ns:(i + 1) * ns]

    n_down = D_MODEL // MXU_DIM
    if skew:
        f_up(0)
        m_input()
        m_glu()
        m_conv(0)
        for c in range(1, N_FF):
            f_up(c)
            m_conv(c)
        for n, stage in zip(range(n_down), (m_gn, m_lconv, m_gates, f_out)):
            f_down(n)
            stage()
        m_scan()
        m_out()
    else:
        m_input()
        m_glu()
        for c in range(N_FF):
            m_conv(c)
        for stage in (m_gn, m_lconv, m_gates, m_scan, m_out):
            stage()
        for c in range(N_FF):
            f_up(c)
        for n in range(n_down):
            f_down(n)
        f_out()

    @pl.when(t == mixer_last)
    def _():
        _store_hist(o_cdw, hist_dw, h_dw, D_CONV, ns, chained)
        _store_hist(o_lc, hist_lc, h_lc, D_LRU, ns, chained)
        o_h[0] = hst[0:o_h.shape[1], :]

    @pl.when(t == ffn_last)
    def _():
        _store_hist(o_fc, hist_fc, h_fc, D_FF, ns, chained)


def _const_spec(shape):
    zeros = (0,) * len(shape)
    return pl.BlockSpec(shape, lambda g, t: zeros, pipeline_mode=pl.Buffered(1))


def _run_layer(x, states, wts, ln0, ns, ni, chained, skew, start_pos, apply_ln0, name):
    n_groups, n_t = x.shape[0], x.shape[1]
    n_steps = n_t + 1 if skew else n_t
    n_rows = ns * ni
    f32 = jnp.float32
    bf16 = jnp.bfloat16
    x2 = x.reshape(n_groups * n_t, ns, ni * D_MODEL)

    def state_spec(arr):
        return pl.BlockSpec((1,) + arr.shape[1:], lambda g, t: (g, 0, 0))

    x_blk = (1, ns, ni * D_MODEL)
    if skew:
        x_spec = pl.BlockSpec(x_blk, lambda g, t: (g * n_t + jnp.minimum(t, n_t - 1), 0, 0))
        y_spec = pl.BlockSpec(x_blk, lambda g, t: (g * n_t + jnp.maximum(t - 1, 0), 0, 0))
    else:
        x_spec = y_spec = pl.BlockSpec(x_blk, lambda g, t: (g * n_t + t, 0, 0))
    operands = [x2] + list(states) + [ln0[0], ln0[1]] + list(wts)
    in_specs = [x_spec] + [state_spec(a) for a in states]
    in_specs += [_const_spec(a.shape) for a in operands[5:]]
    out_shape = [jax.ShapeDtypeStruct(x2.shape, f32)] + [jax.ShapeDtypeStruct(a.shape, f32) for a in states]
    out_specs = [y_spec] + [state_spec(a) for a in states]
    h_dw, h_lc, h_fc = CONV_W - 1, LRU_CONV_W - 1, FFN_CONV_W - 1
    scratch = [
        pltpu.VMEM((h_dw * ns, D_CONV), f32),
        pltpu.VMEM((h_lc * ns, D_LRU), f32),
        pltpu.VMEM((h_fc * ns, D_FF), f32),
        pltpu.VMEM((ns, D_LRU), f32),
        pltpu.VMEM(((h_dw + ni) * ns, D_CONV), f32),
        pltpu.VMEM(((h_lc + ni) * ns, D_LRU), f32),
        pltpu.VMEM(((h_fc + ni) * ns, D_FF), f32),
        pltpu.VMEM((n_rows, D_MODEL), f32),
        pltpu.VMEM((n_rows, D_MODEL), bf16),
        pltpu.VMEM((n_rows, D_CONV), f32),
        pltpu.VMEM((n_rows, D_MODEL), bf16),
        pltpu.VMEM((n_rows, D_MODEL), f32),
        pltpu.VMEM((n_rows, D_MODEL), bf16),
        pltpu.VMEM((n_rows, D_FF), bf16),
    ]
    y, *new_states = pl.pallas_call(
        functools.partial(_layer_kernel, ns, ni, chained, skew, start_pos, apply_ln0),
        grid=(n_groups, n_steps),
        in_specs=in_specs,
        out_specs=out_specs,
        out_shape=out_shape,
        scratch_shapes=scratch,
        compiler_params=pltpu.CompilerParams(
            dimension_semantics=("arbitrary", "arbitrary"),
            vmem_limit_bytes=VMEM_LIMIT_BYTES),
        name=name,
    )(*operands)
    return y.reshape(x.shape), new_states


def _block_diag_gates(w):
    hd = D_LRU // LRU_HEADS
    per = MXU_DIM // hd
    w4 = w.reshape(D_LRU // MXU_DIM, per, hd, hd)
    eye = jnp.eye(per, dtype=w.dtype)
    return jnp.einsum('xhde,hg->xhdge', w4, eye).reshape(D_LRU // MXU_DIM, MXU_DIM, MXU_DIM).astype(jnp.bfloat16)


def _rep8(w):
    return jnp.broadcast_to(w[:, None, :], (w.shape[0], SUBLANES, w.shape[1]))


def _layer_weights(l, w_in, b_in, conv_dw_w, conv_dw_b, conv_gn_g, conv_gn_b, lru_conv_w, lru_conv_b,
                   lru_wa, lru_ba, lru_wx, lru_bx, lru_lambda, w_out, b_out, ln1_g, ln1_b,
                   ffn_w_up, ffn_conv_w, ffn_conv_b, ffn_w_down, ln2_g, ln2_b):
    bf16 = jnp.bfloat16
    row = lambda v: v.reshape(1, -1)
    grp = D_CONV // CONV_GROUPS
    gidx = jnp.arange(MXU_DIM) // grp
    gmat = jnp.where(gidx[:, None] == gidx[None, :], 1.0 / grp, 0.0).astype(bf16)
    return (
        w_in[l].astype(bf16), row(b_in[l]),
        _rep8(conv_dw_w[l]), row(conv_dw_b[l]), row(conv_gn_g[l]), row(conv_gn_b[l]), gmat,
        lru_conv_w[l], row(lru_conv_b[l]),
        _block_diag_gates(lru_wa[l]), _block_diag_gates(lru_wx[l]),
        row(lru_ba[l]), row(lru_bx[l]), row(lru_lambda[l]),
        w_out[l].astype(bf16), row(b_out[l]), row(ln1_g[l]), row(ln1_b[l]),
        ffn_w_up[l].astype(bf16), ffn_conv_w[l], row(ffn_conv_b[l]),
        ffn_w_down[l].astype(bf16), row(ln2_g[l]), row(ln2_b[l]),
    )


def kernel(x_prompt, x_sample, state_conv_dw, state_lru_conv, state_lru_h, state_ffn_conv, ln0_g, ln0_b, w_in, b_in, conv_dw_w, conv_dw_b, conv_gn_g, conv_gn_b, lru_conv_w, lru_conv_b, lru_wa, lru_ba, lru_wx, lru_bx, lru_lambda, w_out, b_out, ln1_g, ln1_b, ffn_w_up, ffn_conv_w, ffn_conv_b, ffn_w_down, ln2_g, ln2_b):
    f32 = jnp.float32
    nb, seq, _ = x_prompt.shape
    db, dseq, _ = x_sample.shape
    ln0 = (ln0_g.reshape(1, -1), ln0_b.reshape(1, -1))
    tile = PROMPT_NS * PROMPT_NI
    p_in = (jnp.zeros((nb, 1, (CONV_W - 1) * D_CONV), f32), jnp.zeros((nb, 1, (LRU_CONV_W - 1) * D_LRU), f32),
            jnp.zeros((nb, 1, D_LRU), f32), jnp.zeros((nb, 1, (FFN_CONV_W - 1) * D_FF), f32))
    xp = x_prompt.reshape(nb, seq // tile, PROMPT_NS, PROMPT_NI * D_MODEL)
    xs = x_sample.reshape(1, 1, db, dseq * D_MODEL)
    p_states, s_states = [], []
    for l in range(DEPTH):
        wts = _layer_weights(l, w_in, b_in, conv_dw_w, conv_dw_b, conv_gn_g, conv_gn_b, lru_conv_w, lru_conv_b,
                             lru_wa, lru_ba, lru_wx, lru_bx, lru_lambda, w_out, b_out, ln1_g, ln1_b,
                             ffn_w_up, ffn_conv_w, ffn_conv_b, ffn_w_down, ln2_g, ln2_b)
        xp, pst = _run_layer(xp, p_in, wts, ln0, PROMPT_NS, PROMPT_NI, True, True, 0, l == 0, f"prompt_l{l}")
        p_states.append(pst)
        s_in = (state_conv_dw[l].reshape(1, db, -1), state_lru_conv[l].reshape(1, db, -1),
                state_lru_h[l].reshape(1, db, D_LRU), state_ffn_conv[l].reshape(1, db, -1))
        xs, sst = _run_layer(xs, s_in, wts, ln0, db, dseq, False, False, PAST_LEN, l == 0, f"sample_l{l}")
        s_states.append(sst)

    def stack(states, k, shape):
        return jnp.stack([st[k] for st in states]).reshape((DEPTH,) + shape)

    return (xp.reshape(nb, seq, D_MODEL), xs.reshape(db, dseq, D_MODEL),
            stack(p_states, 0, (nb, CONV_W - 1, D_CONV)), stack(p_states, 1, (nb, LRU_CONV_W - 1, D_LRU)),
            stack(p_states, 2, (nb, D_LRU)), stack(p_states, 3, (nb, FFN_CONV_W - 1, D_FF)),
            stack(s_states, 0, (db, CONV_W - 1, D_CONV)), stack(s_states, 1, (db, LRU_CONV_W - 1, D_LRU)),
            stack(s_states, 2, (db, D_LRU)), stack(s_states, 3, (db, FFN_CONV_W - 1, D_FF)))
```

```python
import functools

import jax
import jax.numpy as jnp
from jax import lax
from jax.experimental import pallas as pl
from jax.experimental.pallas import tpu as pltpu

D_MODEL = 1024
D_CONV = 512
D_LRU = 512
CONV_W = 31
LRU_CONV_W = 4
FFN_CONV_W = 3
CONV_GROUPS = 8
LRU_HEADS = 8
LRU_C = 8.0
D_FF = 2816
DEPTH = 4
PAST_LEN = 4096
ALPHA = (2 * DEPTH) ** 0.25
LN_EPS = 1e-5

SUBLANES = 8
LANES = 128
MXU_DIM = 256
VMEM_LIMIT_BYTES = 60000 * 1024

FF_CHUNK = MXU_DIM
FF_STARTS = tuple(range(0, D_FF, FF_CHUNK))
N_FF = len(FF_STARTS)
PROMPT_NS = SUBLANES
PROMPT_NI = 64


def _layer_norm(x, g, b):
    mu = jnp.mean(x, axis=-1, keepdims=True)
    d = x - mu
    var = jnp.mean(d * d, axis=-1, keepdims=True)
    return d * lax.rsqrt(var + LN_EPS) * g + b


def _sigmoid(x):
    return 0.5 * jnp.tanh(0.5 * x) + 0.5


def _bdot(a, w):
    return jnp.dot(a, w, preferred_element_type=jnp.float32)


def _group_mean(v, gmat):
    vb = v.astype(jnp.bfloat16)
    halves = []
    for h in range(D_CONV // MXU_DIM):
        cs = slice(h * MXU_DIM, (h + 1) * MXU_DIM)
        halves.append(_bdot(vb[:, cs], gmat))
    return jnp.concatenate(halves, axis=1)


def _sublane_iota(n_ch):
    return lax.broadcasted_iota(jnp.int32, (SUBLANES, n_ch), 0)


def _load_hist(hist, st, n_hist, n_ch, ns, chained):
    if chained:
        hist[...] = jnp.zeros(hist.shape, jnp.float32)
    for j in range(n_hist):
        rows = st[0, :, j * n_ch:(j + 1) * n_ch]
        if chained:
            hist[(j + 1) * ns - 1:(j + 1) * ns, :] = rows
        else:
            hist[j * ns:(j + 1) * ns, :] = rows


def _store_hist(out, hist, n_hist, n_ch, ns, chained):
    for j in range(n_hist):
        if chained:
            out[0, :, j * n_ch:(j + 1) * n_ch] = hist[(j + 1) * ns - 1:(j + 1) * ns, :]
        else:
            out[0, :, j * n_ch:(j + 1) * n_ch] = hist[j * ns:(j + 1) * ns, :]


def _fill_head(buf, hist, n_hist, ni, ns, chained, cs):
    for j in range(n_hist):
        old = hist[j * ns:(j + 1) * ns, cs]
        if chained:
            cur = buf[(ni + j) * ns:(ni + j + 1) * ns, cs]
            sub = _sublane_iota(old.shape[1])
            old = pltpu.roll(jnp.where(sub == ns - 1, old, cur), 1, 0)
        buf[j * ns:(j + 1) * ns, cs] = old


def _conv_tile(buf, w_ref, bias, width, n_rows, ns, cs):
    y = bias
    for k in range(width):
        y = y + w_ref[k:k + 1, cs] * buf[k * ns:k * ns + n_rows, cs]
    return y


def _conv_blocks(buf, w8_ref, bias8, width, blocks, ns, n_ch, out_ref):
    for i in blocks:
        for h in range(ns // SUBLANES):
            for c0 in range(0, n_ch, LANES):
                cs = slice(c0, c0 + LANES)
                acc = [bias8[:, cs], None]
                for k in range(width):
                    r0 = (i + k) * ns + h * SUBLANES
                    term = w8_ref[k, :, cs] * buf[r0:r0 + SUBLANES, cs]
                    acc[k % 2] = term if acc[k % 2] is None else acc[k % 2] + term
                out_ref[i * ns + h * SUBLANES:i * ns + (h + 1) * SUBLANES, cs] = acc[0] + acc[1]


def _lru_scan(a, b, hst, ni, ns, chained):
    blk = lambda v, i: v[i * ns:(i + 1) * ns]
    hs, cum = [], []
    if chained:
        h, acum = blk(b, 0), blk(a, 0)
    else:
        h = blk(a, 0) * hst[...] + blk(b, 0)
    hs.append(h)
    if chained:
        cum.append(acum)
    for i in range(1, ni):
        h = blk(a, i) * h + blk(b, i)
        hs.append(h)
        if chained:
            acum = blk(a, i) * acum
            cum.append(acum)
    if not chained:
        hst[...] = hs[-1]
        return jnp.concatenate(hs, axis=0)
    sub = _sublane_iota(D_LRU)
    a_inc, b_inc = cum[-1], hs[-1]
    for sft in (1, 2, 4):
        a_sh = jnp.where(sub >= sft, pltpu.roll(a_inc, sft, 0), 1.0)
        b_sh = jnp.where(sub >= sft, pltpu.roll(b_inc, sft, 0), 0.0)
        b_inc = a_inc * b_sh + b_inc
        a_inc = a_inc * a_sh
    h_prev = hst[...]
    seg_end = b_inc + a_inc * h_prev
    carry = jnp.where(sub == 0, h_prev, pltpu.roll(seg_end, 1, 0))
    hst[...] = jnp.broadcast_to(seg_end[ns - 1:ns, :], (ns, D_LRU))
    return jnp.concatenate([hl + ac * carry for hl, ac in zip(hs, cum)], axis=0)


def _layer_kernel(
        ns, ni, n_t, chained, skew, start_pos, apply_ln0,
        x_ref, st_cdw, st_lc, st_h, st_fc, ln0_g, ln0_b,
        w_in, b_in, cw, cb, gn_g, gn_b, gmat,
        lw, lb, wa, wx, ba, bx, lam,
        w_out, b_out, ln1_g, ln1_b,
        w_up, fw, fb, w_down, ln2_g, ln2_b,
        y_ref, o_cdw, o_lc, o_h, o_fc,
        hist_dw, hist_lc, hist_fc, hst, cbuf, lbuf, ubuf, xs, xb_s, c_s, mixin_s, x1_s, x1b_s, hid_s):
    u = pl.program_id(1)
    n_rows = ni * ns
    h_dw, h_lc, h_fc = CONV_W - 1, LRU_CONV_W - 1, FFN_CONV_W - 1
    f32 = jnp.float32
    bf16 = jnp.bfloat16
    if chained:
        assert ns == SUBLANES and ni >= h_dw
    t = lax.rem(u, n_t)
    t_ffn = lax.rem(u - 1, n_t) if skew else t

    if skew:
        @pl.when(u == 0)
        def _():
            hist_fc[...] = jnp.zeros(hist_fc.shape, f32)

            def zero_rows(i, carry):
                rows = pl.ds(pl.multiple_of(i * 2 * SUBLANES, 2 * SUBLANES), 2 * SUBLANES)
                x1_s[rows, :] = jnp.zeros((2 * SUBLANES, D_MODEL), f32)
                x1b_s[rows, :] = jnp.zeros((2 * SUBLANES, D_MODEL), bf16)
                return carry

            lax.fori_loop(0, n_rows // (2 * SUBLANES), zero_rows, 0)

    @pl.when(t == 0)
    def _():
        _load_hist(hist_dw, st_cdw, h_dw, D_CONV, ns, chained)
        _load_hist(hist_lc, st_lc, h_lc, D_LRU, ns, chained)
        hst[...] = jnp.broadcast_to(st_h[0], (ns, D_LRU))

    @pl.when(t_ffn == 0)
    def _():
        _load_hist(hist_fc, st_fc, h_fc, D_FF, ns, chained)

    v = {}

    def m_input():
        x = jnp.concatenate([x_ref[0, :, i * D_MODEL:(i + 1) * D_MODEL] for i in range(ni)], axis=0)
        if apply_ln0:
            x = _layer_norm(x, ln0_g[...], ln0_b[...])
        xs[...] = x
        xb_s[...] = x.astype(bf16)

    def m_glu():
        za = _bdot(xb_s[...], w_in[:, 0:D_CONV]) + b_in[:, 0:D_CONV]
        zg = _bdot(xb_s[...], w_in[:, D_CONV:2 * D_CONV]) + b_in[:, D_CONV:2 * D_CONV]
        cbuf[h_dw * ns:, :] = za * _sigmoid(zg)
        _fill_head(cbuf, hist_dw, h_dw, ni, ns, chained, slice(None))

    def m_conv(part):
        blocks = range(part * ni // N_FF, (part + 1) * ni // N_FF)
        _conv_blocks(cbuf, cw, jnp.broadcast_to(cb[...], (SUBLANES, D_CONV)), CONV_W, blocks, ns, D_CONV, c_s)
        if part == N_FF - 1:
            hist_dw[...] = cbuf[ni * ns:, :]

    def m_gn():
        c = c_s[...]
        d = c - _group_mean(c, gmat[...])
        var = _group_mean(d * d, gmat[...])
        cn = d * lax.rsqrt(var + LN_EPS) * gn_g[...] + gn_b[...]
        mixin_s[:, 0:D_CONV] = (cn * _sigmoid(cn)).astype(bf16)

    def m_lconv():
        lbuf[h_lc * ns:, :] = (_bdot(xb_s[...], w_in[:, 2 * D_CONV:2 * D_CONV + D_LRU])
                               + b_in[:, 2 * D_CONV:2 * D_CONV + D_LRU])
        _fill_head(lbuf, hist_lc, h_lc, ni, ns, chained, slice(None))
        v["xl"] = _conv_tile(lbuf, lw, lb[...], LRU_CONV_W, n_rows, ns, slice(None))
        hist_lc[...] = lbuf[ni * ns:, :]

    def m_gates():
        xl = v["xl"]
        xlb = xl.astype(bf16)
        ra, ix = [], []
        for h in range(D_LRU // MXU_DIM):
            cs = slice(h * MXU_DIM, (h + 1) * MXU_DIM)
            ra.append(_bdot(xlb[:, cs], wa[h]))
            ix.append(_bdot(xlb[:, cs], wx[h]))
        r = _sigmoid(jnp.concatenate(ra, axis=1) + ba[...])
        gate_i = _sigmoid(jnp.concatenate(ix, axis=1) + bx[...])
        log_a = r * (-LRU_C * jax.nn.softplus(-lam[...]))
        a = jnp.exp(log_a)
        mult = jnp.sqrt(-jnp.tanh(log_a) * (a * a + 1.0))
        if start_pos == 0:
            row = lax.broadcasted_iota(jnp.int32, (n_rows, D_LRU), 0)
            first_rows = 1 if chained else ns
            reset = jnp.logical_and(row < first_rows, t == 0)
            mult = jnp.where(reset, 1.0, mult)
            a = jnp.where(reset, 0.0, a)
        v["a"], v["b"] = a, mult * (gate_i * xl)

    def m_scan():
        hseq = _lru_scan(v["a"], v["b"], hst, ni, ns, chained)
        zgate = _bdot(xb_s[...], w_in[:, 2 * D_CONV + D_LRU:]) + b_in[:, 2 * D_CONV + D_LRU:]
        mixin_s[:, D_CONV:] = (jax.nn.gelu(zgate) * hseq).astype(bf16)

    def m_out():
        mix = _bdot(mixin_s[...], w_out[...]) + b_out[...]
        x1 = _layer_norm(ALPHA * xs[...] + mix, ln1_g[...], ln1_b[...])
        x1_s[...] = x1
        x1b_s[...] = x1.astype(bf16)

    def f_up(c):
        lo, hi = FF_STARTS[c], min(FF_STARTS[c] + FF_CHUNK, D_FF)
        cs = slice(lo, hi)
        ubuf[h_fc * ns:, cs] = _bdot(x1b_s[...], w_up[:, cs])
        _fill_head(ubuf, hist_fc, h_fc, ni, ns, chained, cs)
        g = _bdot(x1b_s[...], w_up[:, D_FF + lo:D_FF + hi])
        vc = _conv_tile(ubuf, fw, fb[:, cs], FFN_CONV_W, n_rows, ns, cs)
        hid_s[:, cs] = (jax.nn.gelu(vc) * g).astype(bf16)
        if c == N_FF - 1:
            hist_fc[...] = ubuf[ni * ns:, :]

    def f_down(n):
        v["ffn%d" % n] = _bdot(hid_s[...], w_down[:, n * MXU_DIM:(n + 1) * MXU_DIM])

    def f_out():
        ffn = jnp.concatenate([v["ffn%d" % n] for n in range(D_MODEL // MXU_DIM)], axis=1)
        y = _layer_norm(ALPHA * x1_s[...] + ffn, ln2_g[...], ln2_b[...])
        for i in range(ni):
            y_ref[0, :, i * D_MODEL:(i + 1) * D_MODEL] = y[i * ns:(i + 1) * ns]

    n_down = D_MODEL // MXU_DIM
    if skew:
        f_up(0)
        m_input()
        m_glu()
        m_conv(0)
        for c in range(1, N_FF):
            f_up(c)
            m_conv(c)
        for n, stage in zip(range(n_down), (m_gn, m_lconv, m_gates, f_out)):
            f_down(n)
            stage()
        m_scan()
        m_out()
    else:
        m_input()
        m_glu()
        for c in range(N_FF):
            m_conv(c)
        for stage in (m_gn, m_lconv, m_gates, m_scan, m_out):
            stage()
        for c in range(N_FF):
            f_up(c)
        for n in range(n_down):
            f_down(n)
        f_out()

    n_real = pl.num_programs(1) - (1 if skew else 0)

    @pl.when(jnp.logical_and(t == n_t - 1, u < n_real))
    def _():
        _store_hist(o_cdw, hist_dw, h_dw, D_CONV, ns, chained)
        _store_hist(o_lc, hist_lc, h_lc, D_LRU, ns, chained)
        o_h[0] = hst[0:o_h.shape[1], :]

    @pl.when(t_ffn == n_t - 1)
    def _():
        _store_hist(o_fc, hist_fc, h_fc, D_FF, ns, chained)


def _const_spec(shape):
    zeros = (0,) * len(shape)
    return pl.BlockSpec(shape, lambda g, t: zeros, pipeline_mode=pl.Buffered(1))


def _run_layer(x, states, wts, ln0, ns, ni, chained, skew, start_pos, apply_ln0, name):
    n_groups, n_t = x.shape[0], x.shape[1]
    n_tiles = n_groups * n_t
    n_steps = n_tiles + 1 if skew else n_tiles
    n_rows = ns * ni
    f32 = jnp.float32
    bf16 = jnp.bfloat16
    x2 = x.reshape(n_tiles, ns, ni * D_MODEL)

    def mixer_tile(u):
        return jnp.minimum(u, n_tiles - 1)

    def ffn_tile(u):
        return jnp.maximum(u - 1, 0) if skew else u

    def state_spec(arr, tile_of):
        return pl.BlockSpec((1,) + arr.shape[1:], lambda g, u: (tile_of(u) // n_t, 0, 0))

    x_blk = (1, ns, ni * D_MODEL)
    x_spec = pl.BlockSpec(x_blk, lambda g, u: (mixer_tile(u), 0, 0))
    y_spec = pl.BlockSpec(x_blk, lambda g, u: (ffn_tile(u), 0, 0))
    state_tiles = (mixer_tile, mixer_tile, mixer_tile, ffn_tile)
    operands = [x2] + list(states) + [ln0[0], ln0[1]] + list(wts)
    in_specs = [x_spec] + [state_spec(a, f) for a, f in zip(states, state_tiles)]
    in_specs += [_const_spec(a.shape) for a in operands[5:]]
    out_shape = [jax.ShapeDtypeStruct(x2.shape, f32)] + [jax.ShapeDtypeStruct(a.shape, f32) for a in states]
    out_specs = [y_spec] + [state_spec(a, f) for a, f in zip(states, state_tiles)]
    h_dw, h_lc, h_fc = CONV_W - 1, LRU_CONV_W - 1, FFN_CONV_W - 1
    scratch = [
        pltpu.VMEM((h_dw * ns, D_CONV), f32),
        pltpu.VMEM((h_lc * ns, D_LRU), f32),
        pltpu.VMEM((h_fc * ns, D_FF), f32),
        pltpu.VMEM((ns, D_LRU), f32),
        pltpu.VMEM(((h_dw + ni) * ns, D_CONV), f32),
        pltpu.VMEM(((h_lc + ni) * ns, D_LRU), f32),
        pltpu.VMEM(((h_fc + ni) * ns, D_FF), f32),
        pltpu.VMEM((n_rows, D_MODEL), f32),
        pltpu.VMEM((n_rows, D_MODEL), bf16),
        pltpu.VMEM((n_rows, D_CONV), f32),
        pltpu.VMEM((n_rows, D_MODEL), bf16),
        pltpu.VMEM((n_rows, D_MODEL), f32),
        pltpu.VMEM((n_rows, D_MODEL), bf16),
        pltpu.VMEM((n_rows, D_FF), bf16),
    ]
    y, *new_states = pl.pallas_call(
        functools.partial(_layer_kernel, ns, ni, n_t, chained, skew, start_pos, apply_ln0),
        grid=(1, n_steps),
        in_specs=in_specs,
        out_specs=out_specs,
        out_shape=out_shape,
        scratch_shapes=scratch,
        compiler_params=pltpu.CompilerParams(
            dimension_semantics=("arbitrary", "arbitrary"),
            vmem_limit_bytes=VMEM_LIMIT_BYTES),
        name=name,
    )(*operands)
    return y.reshape(x.shape), new_states


def _block_diag_gates(w):
    hd = D_LRU // LRU_HEADS
    per = MXU_DIM // hd
    w4 = w.reshape(D_LRU // MXU_DIM, per, hd, hd)
    eye = jnp.eye(per, dtype=w.dtype)
    return jnp.einsum('xhde,hg->xhdge', w4, eye).reshape(D_LRU // MXU_DIM, MXU_DIM, MXU_DIM).astype(jnp.bfloat16)


def _rep8(w):
    return jnp.broadcast_to(w[:, None, :], (w.shape[0], SUBLANES, w.shape[1]))


def _layer_weights(l, w_in, b_in, conv_dw_w, conv_dw_b, conv_gn_g, conv_gn_b, lru_conv_w, lru_conv_b,
                   lru_wa, lru_ba, lru_wx, lru_bx, lru_lambda, w_out, b_out, ln1_g, ln1_b,
                   ffn_w_up, ffn_conv_w, ffn_conv_b, ffn_w_down, ln2_g, ln2_b):
    bf16 = jnp.bfloat16
    row = lambda v: v.reshape(1, -1)
    grp = D_CONV // CONV_GROUPS
    gidx = jnp.arange(MXU_DIM) // grp
    gmat = jnp.where(gidx[:, None] == gidx[None, :], 1.0 / grp, 0.0).astype(bf16)
    return (
        w_in[l].astype(bf16), row(b_in[l]),
        _rep8(conv_dw_w[l]), row(conv_dw_b[l]), row(conv_gn_g[l]), row(conv_gn_b[l]), gmat,
        lru_conv_w[l], row(lru_conv_b[l]),
        _block_diag_gates(lru_wa[l]), _block_diag_gates(lru_wx[l]),
        row(lru_ba[l]), row(lru_bx[l]), row(lru_lambda[l]),
        w_out[l].astype(bf16), row(b_out[l]), row(ln1_g[l]), row(ln1_b[l]),
        ffn_w_up[l].astype(bf16), ffn_conv_w[l], row(ffn_conv_b[l]),
        ffn_w_down[l].astype(bf16), row(ln2_g[l]), row(ln2_b[l]),
    )


def kernel(x_prompt, x_sample, state_conv_dw, state_lru_conv, state_lru_h, state_ffn_conv, ln0_g, ln0_b, w_in, b_in, conv_dw_w, conv_dw_b, conv_gn_g, conv_gn_b, lru_conv_w, lru_conv_b, lru_wa, lru_ba, lru_wx, lru_bx, lru_lambda, w_out, b_out, ln1_g, ln1_b, ffn_w_up, ffn_conv_w, ffn_conv_b, ffn_w_down, ln2_g, ln2_b):
    f32 = jnp.float32
    nb, seq, _ = x_prompt.shape
    db, dseq, _ = x_sample.shape
    ln0 = (ln0_g.reshape(1, -1), ln0_b.reshape(1, -1))
    tile = PROMPT_NS * PROMPT_NI
    p_in = (jnp.zeros((nb, 1, (CONV_W - 1) * D_CONV), f32), jnp.zeros((nb, 1, (LRU_CONV_W - 1) * D_LRU), f32),
            jnp.zeros((nb, 1, D_LRU), f32), jnp.zeros((nb, 1, (FFN_CONV_W - 1) * D_FF), f32))
    xp = x_prompt.reshape(nb, seq // tile, PROMPT_NS, PROMPT_NI * D_MODEL)
    xs = x_sample.reshape(1, 1, db, dseq * D_MODEL)
    p_states, s_states = [], []
    for l in range(DEPTH):
        wts = _layer_weights(l, w_in, b_in, conv_dw_w, conv_dw_b, conv_gn_g, conv_gn_b, lru_conv_w, lru_conv_b,
                             lru_wa, lru_ba, lru_wx, lru_bx, lru_lambda, w_out, b_out, ln1_g, ln1_b,
                             ffn_w_up, ffn_conv_w, ffn_conv_b, ffn_w_down, ln2_g, ln2_b)
        xp, pst = _run_layer(xp, p_in, wts, ln0, PROMPT_NS, PROMPT_NI, True, True, 0, l == 0, f"prompt_l{l}")
        p_states.append(pst)
        s_in = (state_conv_dw[l].reshape(1, db, -1), state_lru_conv[l].reshape(1, db, -1),
                state_lru_h[l].reshape(1, db, D_LRU), state_ffn_conv[l].reshape(1, db, -1))
        xs, sst = _run_layer(xs, s_in, wts, ln0, db, dseq, False, False, PAST_LEN, l == 0, f"sample_l{l}")
        s_states.append(sst)

    def stack(states, k, shape):
        return jnp.stack([st[k] for st in states]).reshape((DEPTH,) + shape)

    return (xp.reshape(nb, seq, D_MODEL), xs.reshape(db, dseq, D_MODEL),
            stack(p_states, 0, (nb, CONV_W - 1, D_CONV)), stack(p_states, 1, (nb, LRU_CONV_W - 1, D_LRU)),
            stack(p_states, 2, (nb, D_LRU)), stack(p_states, 3, (nb, FFN_CONV_W - 1, D_FF)),
            stack(s_states, 0, (db, CONV_W - 1, D_CONV)), stack(s_states, 1, (db, LRU_CONV_W - 1, D_LRU)),
            stack(s_states, 2, (db, D_LRU)), stack(s_states, 3, (db, FFN_CONV_W - 1, D_FF)))
```

```python
import functools

import jax
import jax.numpy as jnp
from jax import lax
from jax.experimental import pallas as pl
from jax.experimental.pallas import tpu as pltpu

D_MODEL = 1024
D_CONV = 512
D_LRU = 512
CONV_W = 31
LRU_CONV_W = 4
FFN_CONV_W = 3
CONV_GROUPS = 8
LRU_HEADS = 8
LRU_C = 8.0
D_FF = 2816
DEPTH = 4
PAST_LEN = 4096
ALPHA = (2 * DEPTH) ** 0.25
LN_EPS = 1e-5

SUBLANES = 8
LANES = 128
MXU_DIM = 256
VMEM_LIMIT_BYTES = 60000 * 1024

FF_CHUNK = MXU_DIM
FF_STARTS = tuple(range(0, D_FF, FF_CHUNK))
N_FF = len(FF_STARTS)
PROMPT_NS = SUBLANES
PROMPT_NI = 64


def _layer_norm(x, g, b):
    mu = jnp.mean(x, axis=-1, keepdims=True)
    d = x - mu
    var = jnp.mean(d * d, axis=-1, keepdims=True)
    return d * lax.rsqrt(var + LN_EPS) * g + b


def _sigmoid(x):
    return 0.5 * jnp.tanh(0.5 * x) + 0.5


def _bdot(a, w):
    return jnp.dot(a, w, preferred_element_type=jnp.float32)


def _split_bf16(v):
    hi = v.astype(jnp.bfloat16)
    lo = (v - hi.astype(jnp.float32)).astype(jnp.bfloat16)
    return hi, lo


def _group_mean(v, gmat):
    hi, lo = _split_bf16(v)
    halves = []
    for h in range(D_CONV // MXU_DIM):
        cs = slice(h * MXU_DIM, (h + 1) * MXU_DIM)
        halves.append(_bdot(hi[:, cs], gmat) + _bdot(lo[:, cs], gmat))
    return jnp.concatenate(halves, axis=1)


def _sublane_iota(n_ch):
    return lax.broadcasted_iota(jnp.int32, (SUBLANES, n_ch), 0)


def _load_hist(hist, st, n_hist, n_ch, ns, chained):
    if chained:
        hist[...] = jnp.zeros(hist.shape, jnp.float32)
    for j in range(n_hist):
        rows = st[0, :, j * n_ch:(j + 1) * n_ch]
        if chained:
            hist[(j + 1) * ns - 1:(j + 1) * ns, :] = rows
        else:
            hist[j * ns:(j + 1) * ns, :] = rows


def _store_hist(out, hist, n_hist, n_ch, ns, chained):
    for j in range(n_hist):
        if chained:
            out[0, :, j * n_ch:(j + 1) * n_ch] = hist[(j + 1) * ns - 1:(j + 1) * ns, :]
        else:
            out[0, :, j * n_ch:(j + 1) * n_ch] = hist[j * ns:(j + 1) * ns, :]


def _fill_head(buf, hist, n_hist, ni, ns, chained, cs):
    for j in range(n_hist):
        old = hist[j * ns:(j + 1) * ns, cs]
        if chained:
            cur = buf[(ni + j) * ns:(ni + j + 1) * ns, cs]
            sub = _sublane_iota(old.shape[1])
            old = pltpu.roll(jnp.where(sub == ns - 1, old, cur), 1, 0)
        buf[j * ns:(j + 1) * ns, cs] = old


def _conv_tile(buf, w_ref, bias, width, n_rows, ns, cs):
    y = bias
    for k in range(width):
        y = y + w_ref[k:k + 1, cs] * buf[k * ns:k * ns + n_rows, cs]
    return y


def _conv_blocks(buf, w8_ref, bias8, width, blocks, ns, n_ch, out_ref):
    for i in blocks:
        for h in range(ns // SUBLANES):
            for c0 in range(0, n_ch, LANES):
                cs = slice(c0, c0 + LANES)
                acc = [bias8[:, cs], None]
                for k in range(width):
                    r0 = (i + k) * ns + h * SUBLANES
                    term = w8_ref[k, :, cs] * buf[r0:r0 + SUBLANES, cs]
                    acc[k % 2] = term if acc[k % 2] is None else acc[k % 2] + term
                out_ref[i * ns + h * SUBLANES:i * ns + (h + 1) * SUBLANES, cs] = acc[0] + acc[1]


def _lru_scan(a, b, hst, ni, ns, chained):
    blk = lambda v, i: v[i * ns:(i + 1) * ns]
    hs, cum = [], []
    if chained:
        h, acum = blk(b, 0), blk(a, 0)
    else:
        h = blk(a, 0) * hst[...] + blk(b, 0)
    hs.append(h)
    if chained:
        cum.append(acum)
    for i in range(1, ni):
        h = blk(a, i) * h + blk(b, i)
        hs.append(h)
        if chained:
            acum = blk(a, i) * acum
            cum.append(acum)
    if not chained:
        hst[...] = hs[-1]
        return jnp.concatenate(hs, axis=0)
    sub = _sublane_iota(D_LRU)
    a_inc, b_inc = cum[-1], hs[-1]
    for sft in (1, 2, 4):
        a_sh = jnp.where(sub >= sft, pltpu.roll(a_inc, sft, 0), 1.0)
        b_sh = jnp.where(sub >= sft, pltpu.roll(b_inc, sft, 0), 0.0)
        b_inc = a_inc * b_sh + b_inc
        a_inc = a_inc * a_sh
    h_prev = hst[...]
    seg_end = b_inc + a_inc * h_prev
    carry = jnp.where(sub == 0, h_prev, pltpu.roll(seg_end, 1, 0))
    hst[...] = jnp.broadcast_to(seg_end[ns - 1:ns, :], (ns, D_LRU))
    return jnp.concatenate([hl + ac * carry for hl, ac in zip(hs, cum)], axis=0)


def _layer_kernel(
        ns, ni, n_t, chained, skew, start_pos, apply_ln0,
        x_ref, st_cdw, st_lc, st_h, st_fc, ln0_g, ln0_b,
        w_in, b_in, cw, cb, gn_g, gn_b, gmat,
        lw, lb, wa, wx, ba, bx, lam,
        w_out, b_out, ln1_g, ln1_b,
        w_up, fw, fb, w_down, ln2_g, ln2_b,
        y_ref, o_cdw, o_lc, o_h, o_fc,
        hist_dw, hist_lc, hist_fc, hst, cbuf, lbuf, ubuf, xs, xb_s, c_s, mixin_s, x1_s, x1b_s, hid_s):
    u = pl.program_id(1)
    n_rows = ni * ns
    h_dw, h_lc, h_fc = CONV_W - 1, LRU_CONV_W - 1, FFN_CONV_W - 1
    f32 = jnp.float32
    bf16 = jnp.bfloat16
    if chained:
        assert ns == SUBLANES and ni >= h_dw
    t = lax.rem(u, n_t)
    t_ffn = lax.rem(u - 1, n_t) if skew else t

    if skew:
        @pl.when(u == 0)
        def _():
            hist_fc[...] = jnp.zeros(hist_fc.shape, f32)

            def zero_rows(i, carry):
                rows = pl.ds(pl.multiple_of(i * 2 * SUBLANES, 2 * SUBLANES), 2 * SUBLANES)
                x1_s[rows, :] = jnp.zeros((2 * SUBLANES, D_MODEL), f32)
                x1b_s[rows, :] = jnp.zeros((2 * SUBLANES, D_MODEL), bf16)
                return carry

            lax.fori_loop(0, n_rows // (2 * SUBLANES), zero_rows, 0)

    @pl.when(t == 0)
    def _():
        _load_hist(hist_dw, st_cdw, h_dw, D_CONV, ns, chained)
        _load_hist(hist_lc, st_lc, h_lc, D_LRU, ns, chained)
        hst[...] = jnp.broadcast_to(st_h[0], (ns, D_LRU))

    @pl.when(t_ffn == 0)
    def _():
        _load_hist(hist_fc, st_fc, h_fc, D_FF, ns, chained)

    v = {}

    def m_input():
        x = jnp.concatenate([x_ref[0, :, i * D_MODEL:(i + 1) * D_MODEL] for i in range(ni)], axis=0)
        if apply_ln0:
            x = _layer_norm(x, ln0_g[...], ln0_b[...])
        xs[...] = x
        xb_s[...] = x.astype(bf16)

    def m_glu():
        za = _bdot(xb_s[...], w_in[:, 0:D_CONV]) + b_in[:, 0:D_CONV]
        zg = _bdot(xb_s[...], w_in[:, D_CONV:2 * D_CONV]) + b_in[:, D_CONV:2 * D_CONV]
        cbuf[h_dw * ns:, :] = za * _sigmoid(zg)
        _fill_head(cbuf, hist_dw, h_dw, ni, ns, chained, slice(None))

    def m_conv(part):
        blocks = range(part * ni // N_FF, (part + 1) * ni // N_FF)
        _conv_blocks(cbuf, cw, jnp.broadcast_to(cb[...], (SUBLANES, D_CONV)), CONV_W, blocks, ns, D_CONV, c_s)
        if part == N_FF - 1:
            hist_dw[...] = cbuf[ni * ns:, :]

    def m_gn():
        c = c_s[...]
        d = c - _group_mean(c, gmat[...])
        var = _group_mean(d * d, gmat[...])
        cn = d * lax.rsqrt(var + LN_EPS) * gn_g[...] + gn_b[...]
        mixin_s[:, 0:D_CONV] = (cn * _sigmoid(cn)).astype(bf16)

    def m_lconv():
        lbuf[h_lc * ns:, :] = (_bdot(xb_s[...], w_in[:, 2 * D_CONV:2 * D_CONV + D_LRU])
                               + b_in[:, 2 * D_CONV:2 * D_CONV + D_LRU])
        _fill_head(lbuf, hist_lc, h_lc, ni, ns, chained, slice(None))
        v["xl"] = _conv_tile(lbuf, lw, lb[...], LRU_CONV_W, n_rows, ns, slice(None))
        hist_lc[...] = lbuf[ni * ns:, :]

    def m_gates():
        xl = v["xl"]
        xlb = xl.astype(bf16)
        ra, ix = [], []
        for h in range(D_LRU // MXU_DIM):
            cs = slice(h * MXU_DIM, (h + 1) * MXU_DIM)
            ra.append(_bdot(xlb[:, cs], wa[h]))
            ix.append(_bdot(xlb[:, cs], wx[h]))
        r = _sigmoid(jnp.concatenate(ra, axis=1) + ba[...])
        gate_i = _sigmoid(jnp.concatenate(ix, axis=1) + bx[...])
        log_a = r * (-LRU_C * jax.nn.softplus(-lam[...]))
        a = jnp.exp(log_a)
        mult = jnp.sqrt(-jnp.tanh(log_a) * (a * a + 1.0))
        if start_pos == 0:
            row = lax.broadcasted_iota(jnp.int32, (n_rows, D_LRU), 0)
            first_rows = 1 if chained else ns
            reset = jnp.logical_and(row < first_rows, t == 0)
            mult = jnp.where(reset, 1.0, mult)
            a = jnp.where(reset, 0.0, a)
        v["a"], v["b"] = a, mult * (gate_i * xl)

    def m_scan():
        hseq = _lru_scan(v["a"], v["b"], hst, ni, ns, chained)
        zgate = _bdot(xb_s[...], w_in[:, 2 * D_CONV + D_LRU:]) + b_in[:, 2 * D_CONV + D_LRU:]
        mixin_s[:, D_CONV:] = (jax.nn.gelu(zgate) * hseq).astype(bf16)

    def m_out():
        mix = _bdot(mixin_s[...], w_out[...]) + b_out[...]
        x1 = _layer_norm(ALPHA * xs[...] + mix, ln1_g[...], ln1_b[...])
        x1_s[...] = x1
        x1b_s[...] = x1.astype(bf16)

    def f_up(c):
        lo, hi = FF_STARTS[c], min(FF_STARTS[c] + FF_CHUNK, D_FF)
        cs = slice(lo, hi)
        ubuf[h_fc * ns:, cs] = _bdot(x1b_s[...], w_up[:, cs])
        _fill_head(ubuf, hist_fc, h_fc, ni, ns, chained, cs)
        g = _bdot(x1b_s[...], w_up[:, D_FF + lo:D_FF + hi])
        vc = _conv_tile(ubuf, fw, fb[:, cs], FFN_CONV_W, n_rows, ns, cs)
        hid_s[:, cs] = (jax.nn.gelu(vc) * g).astype(bf16)
        if c == N_FF - 1:
            hist_fc[...] = ubuf[ni * ns:, :]

    def f_down(n):
        v["ffn%d" % n] = _bdot(hid_s[...], w_down[:, n * MXU_DIM:(n + 1) * MXU_DIM])

    def f_out():
        ffn = jnp.concatenate([v["ffn%d" % n] for n in range(D_MODEL // MXU_DIM)], axis=1)
        y = _layer_norm(ALPHA * x1_s[...] + ffn, ln2_g[...], ln2_b[...])
        for i in range(ni):
            y_ref[0, :, i * D_MODEL:(i + 1) * D_MODEL] = y[i * ns:(i + 1) * ns]

    n_down = D_MODEL // MXU_DIM
    if skew:
        f_up(0)
        m_input()
        m_glu()
        m_conv(0)
        for c in range(1, N_FF):
            f_up(c)
            m_conv(c)
        for n, stage in zip(range(n_down), (m_gn, m_lconv, m_gates, f_out)):
            f_down(n)
            stage()
        m_scan()
        m_out()
    else:
        m_input()
        m_glu()
        for c in range(N_FF):
            m_conv(c)
        for stage in (m_gn, m_lconv, m_gates, m_scan, m_out):
            stage()
        for c in range(N_FF):
            f_up(c)
        for n in range(n_down):
            f_down(n)
        f_out()

    n_real = pl.num_programs(1) - (1 if skew else 0)

    @pl.when(jnp.logical_and(t == n_t - 1, u < n_real))
    def _():
        _store_hist(o_cdw, hist_dw, h_dw, D_CONV, ns, chained)
        _store_hist(o_lc, hist_lc, h_lc, D_LRU, ns, chained)
        o_h[0] = hst[0:o_h.shape[1], :]

    @pl.when(t_ffn == n_t - 1)
    def _():
        _store_hist(o_fc, hist_fc, h_fc, D_FF, ns, chained)


def _const_spec(shape):
    zeros = (0,) * len(shape)
    return pl.BlockSpec(shape, lambda g, t: zeros, pipeline_mode=pl.Buffered(1))


def _run_layer(x, states, wts, ln0, ns, ni, chained, skew, start_pos, apply_ln0, name):
    n_groups, n_t = x.shape[0], x.shape[1]
    n_tiles = n_groups * n_t
    n_steps = n_tiles + 1 if skew else n_tiles
    n_rows = ns * ni
    f32 = jnp.float32
    bf16 = jnp.bfloat16
    x2 = x.reshape(n_tiles, ns, ni * D_MODEL)

    def mixer_tile(u):
        return jnp.minimum(u, n_tiles - 1)

    def ffn_tile(u):
        return jnp.maximum(u - 1, 0) if skew else u

    def state_spec(arr, tile_of):
        return pl.BlockSpec((1,) + arr.shape[1:], lambda g, u: (tile_of(u) // n_t, 0, 0))

    x_blk = (1, ns, ni * D_MODEL)
    x_spec = pl.BlockSpec(x_blk, lambda g, u: (mixer_tile(u), 0, 0))
    y_spec = pl.BlockSpec(x_blk, lambda g, u: (ffn_tile(u), 0, 0))
    state_tiles = (mixer_tile, mixer_tile, mixer_tile, ffn_tile)
    operands = [x2] + list(states) + [ln0[0], ln0[1]] + list(wts)
    in_specs = [x_spec] + [state_spec(a, f) for a, f in zip(states, state_tiles)]
    in_specs += [_const_spec(a.shape) for a in operands[5:]]
    out_shape = [jax.ShapeDtypeStruct(x2.shape, f32)] + [jax.ShapeDtypeStruct(a.shape, f32) for a in states]
    out_specs = [y_spec] + [state_spec(a, f) for a, f in zip(states, state_tiles)]
    h_dw, h_lc, h_fc = CONV_W - 1, LRU_CONV_W - 1, FFN_CONV_W - 1
    scratch = [
        pltpu.VMEM((h_dw * ns, D_CONV), f32),
        pltpu.VMEM((h_lc * ns, D_LRU), f32),
        pltpu.VMEM((h_fc * ns, D_FF), f32),
        pltpu.VMEM((ns, D_LRU), f32),
        pltpu.VMEM(((h_dw + ni) * ns, D_CONV), f32),
        pltpu.VMEM(((h_lc + ni) * ns, D_LRU), f32),
        pltpu.VMEM(((h_fc + ni) * ns, D_FF), f32),
        pltpu.VMEM((n_rows, D_MODEL), f32),
        pltpu.VMEM((n_rows, D_MODEL), bf16),
        pltpu.VMEM((n_rows, D_CONV), f32),
        pltpu.VMEM((n_rows, D_MODEL), bf16),
        pltpu.VMEM((n_rows, D_MODEL), f32),
        pltpu.VMEM((n_rows, D_MODEL), bf16),
        pltpu.VMEM((n_rows, D_FF), bf16),
    ]
    y, *new_states = pl.pallas_call(
        functools.partial(_layer_kernel, ns, ni, n_t, chained, skew, start_pos, apply_ln0),
        grid=(1, n_steps),
        in_specs=in_specs,
        out_specs=out_specs,
        out_shape=out_shape,
        scratch_shapes=scratch,
        compiler_params=pltpu.CompilerParams(
            dimension_semantics=("arbitrary", "arbitrary"),
            vmem_limit_bytes=VMEM_LIMIT_BYTES),
        name=name,
    )(*operands)
    return y.reshape(x.shape), new_states


def _block_diag_gates(w):
    hd = D_LRU // LRU_HEADS
    per = MXU_DIM // hd
    w4 = w.reshape(D_LRU // MXU_DIM, per, hd, hd)
    eye = jnp.eye(per, dtype=w.dtype)
    return jnp.einsum('xhde,hg->xhdge', w4, eye).reshape(D_LRU // MXU_DIM, MXU_DIM, MXU_DIM).astype(jnp.bfloat16)


def _rep8(w):
    return jnp.broadcast_to(w[:, None, :], (w.shape[0], SUBLANES, w.shape[1]))


def _layer_weights(l, w_in, b_in, conv_dw_w, conv_dw_b, conv_gn_g, conv_gn_b, lru_conv_w, lru_conv_b,
                   lru_wa, lru_ba, lru_wx, lru_bx, lru_lambda, w_out, b_out, ln1_g, ln1_b,
                   ffn_w_up, ffn_conv_w, ffn_conv_b, ffn_w_down, ln2_g, ln2_b):
    bf16 = jnp.bfloat16
    row = lambda v: v.reshape(1, -1)
    grp = D_CONV // CONV_GROUPS
    gidx = jnp.arange(MXU_DIM) // grp
    gmat = jnp.where(gidx[:, None] == gidx[None, :], 1.0 / grp, 0.0).astype(bf16)
    return (
        w_in[l].astype(bf16), row(b_in[l]),
        _rep8(conv_dw_w[l]), row(conv_dw_b[l]), row(conv_gn_g[l]), row(conv_gn_b[l]), gmat,
        lru_conv_w[l], row(lru_conv_b[l]),
        _block_diag_gates(lru_wa[l]), _block_diag_gates(lru_wx[l]),
        row(lru_ba[l]), row(lru_bx[l]), row(lru_lambda[l]),
        w_out[l].astype(bf16), row(b_out[l]), row(ln1_g[l]), row(ln1_b[l]),
        ffn_w_up[l].astype(bf16), ffn_conv_w[l], row(ffn_conv_b[l]),
        ffn_w_down[l].astype(bf16), row(ln2_g[l]), row(ln2_b[l]),
    )


def kernel(x_prompt, x_sample, state_conv_dw, state_lru_conv, state_lru_h, state_ffn_conv, ln0_g, ln0_b, w_in, b_in, conv_dw_w, conv_dw_b, conv_gn_g, conv_gn_b, lru_conv_w, lru_conv_b, lru_wa, lru_ba, lru_wx, lru_bx, lru_lambda, w_out, b_out, ln1_g, ln1_b, ffn_w_up, ffn_conv_w, ffn_conv_b, ffn_w_down, ln2_g, ln2_b):
    f32 = jnp.float32
    nb, seq, _ = x_prompt.shape
    db, dseq, _ = x_sample.shape
    ln0 = (ln0_g.reshape(1, -1), ln0_b.reshape(1, -1))
    tile = PROMPT_NS * PROMPT_NI
    p_in = (jnp.zeros((nb, 1, (CONV_W - 1) * D_CONV), f32), jnp.zeros((nb, 1, (LRU_CONV_W - 1) * D_LRU), f32),
            jnp.zeros((nb, 1, D_LRU), f32), jnp.zeros((nb, 1, (FFN_CONV_W - 1) * D_FF), f32))
    xp = x_prompt.reshape(nb, seq // tile, PROMPT_NS, PROMPT_NI * D_MODEL)
    xs = x_sample.reshape(1, 1, db, dseq * D_MODEL)
    p_states, s_states = [], []
    for l in range(DEPTH):
        wts = _layer_weights(l, w_in, b_in, conv_dw_w, conv_dw_b, conv_gn_g, conv_gn_b, lru_conv_w, lru_conv_b,
                             lru_wa, lru_ba, lru_wx, lru_bx, lru_lambda, w_out, b_out, ln1_g, ln1_b,
                             ffn_w_up, ffn_conv_w, ffn_conv_b, ffn_w_down, ln2_g, ln2_b)
        xp, pst = _run_layer(xp, p_in, wts, ln0, PROMPT_NS, PROMPT_NI, True, True, 0, l == 0, f"prompt_l{l}")
        p_states.append(pst)
        s_in = (state_conv_dw[l].reshape(1, db, -1), state_lru_conv[l].reshape(1, db, -1),
                state_lru_h[l].reshape(1, db, D_LRU), state_ffn_conv[l].reshape(1, db, -1))
        xs, sst = _run_layer(xs, s_in, wts, ln0, db, dseq, False, False, PAST_LEN, l == 0, f"sample_l{l}")
        s_states.append(sst)

    def stack(states, k, shape):
        return jnp.stack([st[k] for st in states]).reshape((DEPTH,) + shape)

    return (xp.reshape(nb, seq, D_MODEL), xs.reshape(db, dseq, D_MODEL),
            stack(p_states, 0, (nb, CONV_W - 1, D_CONV)), stack(p_states, 1, (nb, LRU_CONV_W - 1, D_LRU)),
            stack(p_states, 2, (nb, D_LRU)), stack(p_states, 3, (nb, FFN_CONV_W - 1, D_FF)),
            stack(s_states, 0, (db, CONV_W - 1, D_CONV)), stack(s_states, 1, (db, LRU_CONV_W - 1, D_LRU)),
            stack(s_states, 2, (db, D_LRU)), stack(s_states, 3, (db, FFN_CONV_W - 1, D_FF)))
```

```python
import functools

import jax
import jax.numpy as jnp
from jax import lax
from jax.experimental import pallas as pl
from jax.experimental.pallas import tpu as pltpu

D_MODEL = 1024
D_CONV = 512
D_LRU = 512
CONV_W = 31
LRU_CONV_W = 4
FFN_CONV_W = 3
CONV_GROUPS = 8
LRU_HEADS = 8
LRU_C = 8.0
D_FF = 2816
DEPTH = 4
PAST_LEN = 4096
ALPHA = (2 * DEPTH) ** 0.25
LN_EPS = 1e-5

SUBLANES = 8
LANES = 128
MXU_DIM = 256
VMEM_LIMIT_BYTES = 60000 * 1024

FF_CHUNK = MXU_DIM
FF_STARTS = tuple(range(0, D_FF, FF_CHUNK))
N_FF = len(FF_STARTS)
PROMPT_NS = SUBLANES
PROMPT_NI = 64


def _layer_norm(x, g, b):
    mu = jnp.mean(x, axis=-1, keepdims=True)
    d = x - mu
    var = jnp.mean(d * d, axis=-1, keepdims=True)
    return d * lax.rsqrt(var + LN_EPS) * g + b


def _sigmoid(x):
    return 0.5 * jnp.tanh(0.5 * x) + 0.5


def _bdot(a, w):
    return jnp.dot(a, w, preferred_element_type=jnp.float32)


def _split_bf16(v):
    hi = v.astype(jnp.bfloat16)
    lo = (v - hi.astype(jnp.float32)).astype(jnp.bfloat16)
    return hi, lo


def _group_mean(v, gmat):
    hi, lo = _split_bf16(v)
    halves = []
    for h in range(D_CONV // MXU_DIM):
        cs = slice(h * MXU_DIM, (h + 1) * MXU_DIM)
        halves.append(_bdot(hi[:, cs], gmat) + _bdot(lo[:, cs], gmat))
    return jnp.concatenate(halves, axis=1)


def _sublane_iota(n_ch):
    return lax.broadcasted_iota(jnp.int32, (SUBLANES, n_ch), 0)


def _load_hist(hist, st, n_hist, n_ch, ns, chained):
    if chained:
        hist[...] = jnp.zeros(hist.shape, jnp.float32)
    for j in range(n_hist):
        rows = st[0, :, j * n_ch:(j + 1) * n_ch]
        if chained:
            hist[(j + 1) * ns - 1:(j + 1) * ns, :] = rows
        else:
            hist[j * ns:(j + 1) * ns, :] = rows


def _store_hist(out, hist, n_hist, n_ch, ns, chained):
    for j in range(n_hist):
        if chained:
            out[0, :, j * n_ch:(j + 1) * n_ch] = hist[(j + 1) * ns - 1:(j + 1) * ns, :]
        else:
            out[0, :, j * n_ch:(j + 1) * n_ch] = hist[j * ns:(j + 1) * ns, :]


def _fill_head(buf, hist, n_hist, ni, ns, chained, cs):
    for j in range(n_hist):
        old = hist[j * ns:(j + 1) * ns, cs]
        if chained:
            cur = buf[(ni + j) * ns:(ni + j + 1) * ns, cs]
            sub = _sublane_iota(old.shape[1])
            old = pltpu.roll(jnp.where(sub == ns - 1, old, cur), 1, 0)
        buf[j * ns:(j + 1) * ns, cs] = old


def _conv_tile(buf, w_ref, bias, width, n_rows, ns, cs):
    y = bias
    for k in range(width):
        y = y + w_ref[k:k + 1, cs] * buf[k * ns:k * ns + n_rows, cs]
    return y


def _conv_blocks(buf, w8_ref, bias8, width, blocks, ns, n_ch, out_ref):
    for i in blocks:
        for h in range(ns // SUBLANES):
            for c0 in range(0, n_ch, LANES):
                cs = slice(c0, c0 + LANES)
                acc = [bias8[:, cs], None]
                for k in range(width):
                    r0 = (i + k) * ns + h * SUBLANES
                    term = w8_ref[k, :, cs] * buf[r0:r0 + SUBLANES, cs]
                    acc[k % 2] = term if acc[k % 2] is None else acc[k % 2] + term
                out_ref[i * ns + h * SUBLANES:i * ns + (h + 1) * SUBLANES, cs] = acc[0] + acc[1]


def _lru_scan(a, b, hst, ni, ns, chained):
    blk = lambda v, i: v[i * ns:(i + 1) * ns]
    hs, cum = [], []
    if chained:
        h, acum = blk(b, 0), blk(a, 0)
    else:
        h = blk(a, 0) * hst[...] + blk(b, 0)
    hs.append(h)
    if chained:
        cum.append(acum)
    for i in range(1, ni):
        h = blk(a, i) * h + blk(b, i)
        hs.append(h)
        if chained:
            acum = blk(a, i) * acum
            cum.append(acum)
    if not chained:
        hst[...] = hs[-1]
        return jnp.concatenate(hs, axis=0)
    sub = _sublane_iota(D_LRU)
    a_inc, b_inc = cum[-1], hs[-1]
    for sft in (1, 2, 4):
        a_sh = jnp.where(sub >= sft, pltpu.roll(a_inc, sft, 0), 1.0)
        b_sh = jnp.where(sub >= sft, pltpu.roll(b_inc, sft, 0), 0.0)
        b_inc = a_inc * b_sh + b_inc
        a_inc = a_inc * a_sh
    h_prev = hst[...]
    seg_end = b_inc + a_inc * h_prev
    carry = jnp.where(sub == 0, h_prev, pltpu.roll(seg_end, 1, 0))
    hst[...] = jnp.broadcast_to(seg_end[ns - 1:ns, :], (ns, D_LRU))
    return jnp.concatenate([hl + ac * carry for hl, ac in zip(hs, cum)], axis=0)


def _layer_kernel(
        ns, ni, n_t, chained, skew, start_pos, apply_ln0,
        x_ref, st_cdw, st_lc, st_h, st_fc, ln0_g, ln0_b,
        w_in, b_in, cw, cb, gn_g, gn_b, gmat,
        lw, lb, wa, wx, ba, bx, lam,
        w_out, b_out, ln1_g, ln1_b,
        w_up, fw, fb, w_down, ln2_g, ln2_b,
        y_ref, o_cdw, o_lc, o_h, o_fc,
        hist_dw, hist_lc, hist_fc, hst, cbuf, lbuf, ubuf, xs, xb_s, c_s, mixin_s, x1_s, x1b_s, hid_s):
    u = pl.program_id(1)
    n_rows = ni * ns
    h_dw, h_lc, h_fc = CONV_W - 1, LRU_CONV_W - 1, FFN_CONV_W - 1
    f32 = jnp.float32
    bf16 = jnp.bfloat16
    if chained:
        assert ns == SUBLANES and ni >= h_dw
    t = lax.rem(u, n_t)
    t_ffn = lax.rem(u - 1, n_t) if skew else t

    if skew:
        @pl.when(u == 0)
        def _():
            hist_fc[...] = jnp.zeros(hist_fc.shape, f32)

            def zero_rows(i, carry):
                rows = pl.ds(pl.multiple_of(i * 2 * SUBLANES, 2 * SUBLANES), 2 * SUBLANES)
                x1_s[rows, :] = jnp.zeros((2 * SUBLANES, D_MODEL), f32)
                x1b_s[rows, :] = jnp.zeros((2 * SUBLANES, D_MODEL), bf16)
                return carry

            lax.fori_loop(0, n_rows // (2 * SUBLANES), zero_rows, 0)

    @pl.when(t == 0)
    def _():
        _load_hist(hist_dw, st_cdw, h_dw, D_CONV, ns, chained)
        _load_hist(hist_lc, st_lc, h_lc, D_LRU, ns, chained)
        hst[...] = jnp.broadcast_to(st_h[0], (ns, D_LRU))

    @pl.when(t_ffn == 0)
    def _():
        _load_hist(hist_fc, st_fc, h_fc, D_FF, ns, chained)

    v = {}

    def x_tile():
        return jnp.concatenate([x_ref[0, :, i * D_MODEL:(i + 1) * D_MODEL] for i in range(ni)], axis=0)

    def m_input():
        x = x_tile()
        if apply_ln0:
            x = _layer_norm(x, ln0_g[...], ln0_b[...])
            xs[...] = x
        xb_s[...] = x.astype(bf16)

    def m_glu():
        za = _bdot(xb_s[...], w_in[:, 0:D_CONV]) + b_in[:, 0:D_CONV]
        zg = _bdot(xb_s[...], w_in[:, D_CONV:2 * D_CONV]) + b_in[:, D_CONV:2 * D_CONV]
        cbuf[h_dw * ns:, :] = za * _sigmoid(zg)
        _fill_head(cbuf, hist_dw, h_dw, ni, ns, chained, slice(None))

    def m_conv(part):
        blocks = range(part * ni // N_FF, (part + 1) * ni // N_FF)
        _conv_blocks(cbuf, cw, jnp.broadcast_to(cb[...], (SUBLANES, D_CONV)), CONV_W, blocks, ns, D_CONV, c_s)
        if part == N_FF - 1:
            hist_dw[...] = cbuf[ni * ns:, :]

    def m_gn():
        c = c_s[...]
        d = c - _group_mean(c, gmat[...])
        var = _group_mean(d * d, gmat[...])
        cn = d * lax.rsqrt(var + LN_EPS) * gn_g[...] + gn_b[...]
        mixin_s[:, 0:D_CONV] = (cn * _sigmoid(cn)).astype(bf16)

    def m_lconv():
        lbuf[h_lc * ns:, :] = (_bdot(xb_s[...], w_in[:, 2 * D_CONV:2 * D_CONV + D_LRU])
                               + b_in[:, 2 * D_CONV:2 * D_CONV + D_LRU])
        _fill_head(lbuf, hist_lc, h_lc, ni, ns, chained, slice(None))
        v["xl"] = _conv_tile(lbuf, lw, lb[...], LRU_CONV_W, n_rows, ns, slice(None))
        hist_lc[...] = lbuf[ni * ns:, :]

    def m_gates():
        xl = v["xl"]
        xlb = xl.astype(bf16)
        ra, ix = [], []
        for h in range(D_LRU // MXU_DIM):
            cs = slice(h * MXU_DIM, (h + 1) * MXU_DIM)
            ra.append(_bdot(xlb[:, cs], wa[h]))
            ix.append(_bdot(xlb[:, cs], wx[h]))
        r = _sigmoid(jnp.concatenate(ra, axis=1) + ba[...])
        gate_i = _sigmoid(jnp.concatenate(ix, axis=1) + bx[...])
        log_a = r * (-LRU_C * jax.nn.softplus(-lam[...]))
        a = jnp.exp(log_a)
        mult = jnp.sqrt(-jnp.tanh(log_a) * (a * a + 1.0))
        if start_pos == 0:
            row = lax.broadcasted_iota(jnp.int32, (n_rows, D_LRU), 0)
            first_rows = 1 if chained else ns
            reset = jnp.logical_and(row < first_rows, t == 0)
            mult = jnp.where(reset, 1.0, mult)
            a = jnp.where(reset, 0.0, a)
        v["a"], v["b"] = a, mult * (gate_i * xl)

    def m_scan():
        hseq = _lru_scan(v["a"], v["b"], hst, ni, ns, chained)
        zgate = _bdot(xb_s[...], w_in[:, 2 * D_CONV + D_LRU:]) + b_in[:, 2 * D_CONV + D_LRU:]
        mixin_s[:, D_CONV:] = (jax.nn.gelu(zgate) * hseq).astype(bf16)

    def m_out():
        mix = _bdot(mixin_s[...], w_out[...]) + b_out[...]
        x_res = xs[...] if apply_ln0 else x_tile()
        x1 = _layer_norm(ALPHA * x_res + mix, ln1_g[...], ln1_b[...])
        x1_s[...] = x1
        x1b_s[...] = x1.astype(bf16)

    def f_up(c):
        lo, hi = FF_STARTS[c], min(FF_STARTS[c] + FF_CHUNK, D_FF)
        cs = slice(lo, hi)
        ubuf[h_fc * ns:, cs] = _bdot(x1b_s[...], w_up[:, cs])
        _fill_head(ubuf, hist_fc, h_fc, ni, ns, chained, cs)
        g = _bdot(x1b_s[...], w_up[:, D_FF + lo:D_FF + hi])
        vc = _conv_tile(ubuf, fw, fb[:, cs], FFN_CONV_W, n_rows, ns, cs)
        hid_s[:, cs] = (jax.nn.gelu(vc) * g).astype(bf16)
        if c == N_FF - 1:
            hist_fc[...] = ubuf[ni * ns:, :]

    def f_down(n):
        v["ffn%d" % n] = _bdot(hid_s[...], w_down[:, n * MXU_DIM:(n + 1) * MXU_DIM])

    def f_out():
        ffn = jnp.concatenate([v["ffn%d" % n] for n in range(D_MODEL // MXU_DIM)], axis=1)
        y = _layer_norm(ALPHA * x1_s[...] + ffn, ln2_g[...], ln2_b[...])
        for i in range(ni):
            y_ref[0, :, i * D_MODEL:(i + 1) * D_MODEL] = y[i * ns:(i + 1) * ns]

    n_down = D_MODEL // MXU_DIM
    if skew:
        f_up(0)
        m_input()
        m_glu()
        m_conv(0)
        for c in range(1, N_FF):
            f_up(c)
            m_conv(c)
        for n, stage in zip(range(n_down), (m_gn, m_lconv, m_gates, f_out)):
            f_down(n)
            stage()
        m_scan()
        m_out()
    else:
        m_input()
        m_glu()
        for c in range(N_FF):
            m_conv(c)
        for stage in (m_gn, m_lconv, m_gates, m_scan, m_out):
            stage()
        for c in range(N_FF):
            f_up(c)
        for n in range(n_down):
            f_down(n)
        f_out()

    n_real = pl.num_programs(1) - (1 if skew else 0)

    @pl.when(jnp.logical_and(t == n_t - 1, u < n_real))
    def _():
        _store_hist(o_cdw, hist_dw, h_dw, D_CONV, ns, chained)
        _store_hist(o_lc, hist_lc, h_lc, D_LRU, ns, chained)
        o_h[0] = hst[0:o_h.shape[1], :]

    @pl.when(t_ffn == n_t - 1)
    def _():
        _store_hist(o_fc, hist_fc, h_fc, D_FF, ns, chained)


def _const_spec(shape):
    zeros = (0,) * len(shape)
    return pl.BlockSpec(shape, lambda g, t: zeros, pipeline_mode=pl.Buffered(1))


def _run_layer(x, states, wts, ln0, ns, ni, chained, skew, start_pos, apply_ln0, name):
    n_groups, n_t = x.shape[0], x.shape[1]
    n_tiles = n_groups * n_t
    n_steps = n_tiles + 1 if skew else n_tiles
    n_rows = ns * ni
    f32 = jnp.float32
    bf16 = jnp.bfloat16
    x2 = x.reshape(n_tiles, ns, ni * D_MODEL)

    def mixer_tile(u):
        return jnp.minimum(u, n_tiles - 1)

    def ffn_tile(u):
        return jnp.maximum(u - 1, 0) if skew else u

    def state_spec(arr, tile_of):
        return pl.BlockSpec((1,) + arr.shape[1:], lambda g, u: (tile_of(u) // n_t, 0, 0))

    x_blk = (1, ns, ni * D_MODEL)
    x_spec = pl.BlockSpec(x_blk, lambda g, u: (mixer_tile(u), 0, 0))
    y_spec = pl.BlockSpec(x_blk, lambda g, u: (ffn_tile(u), 0, 0))
    state_tiles = (mixer_tile, mixer_tile, mixer_tile, ffn_tile)
    operands = [x2] + list(states) + [ln0[0], ln0[1]] + list(wts)
    in_specs = [x_spec] + [state_spec(a, f) for a, f in zip(states, state_tiles)]
    in_specs += [_const_spec(a.shape) for a in operands[5:]]
    out_shape = [jax.ShapeDtypeStruct(x2.shape, f32)] + [jax.ShapeDtypeStruct(a.shape, f32) for a in states]
    out_specs = [y_spec] + [state_spec(a, f) for a, f in zip(states, state_tiles)]
    h_dw, h_lc, h_fc = CONV_W - 1, LRU_CONV_W - 1, FFN_CONV_W - 1
    scratch = [
        pltpu.VMEM((h_dw * ns, D_CONV), f32),
        pltpu.VMEM((h_lc * ns, D_LRU), f32),
        pltpu.VMEM((h_fc * ns, D_FF), f32),
        pltpu.VMEM((ns, D_LRU), f32),
        pltpu.VMEM(((h_dw + ni) * ns, D_CONV), f32),
        pltpu.VMEM(((h_lc + ni) * ns, D_LRU), f32),
        pltpu.VMEM(((h_fc + ni) * ns, D_FF), f32),
        pltpu.VMEM((n_rows, D_MODEL), f32),
        pltpu.VMEM((n_rows, D_MODEL), bf16),
        pltpu.VMEM((n_rows, D_CONV), f32),
        pltpu.VMEM((n_rows, D_MODEL), bf16),
        pltpu.VMEM((n_rows, D_MODEL), f32),
        pltpu.VMEM((n_rows, D_MODEL), bf16),
        pltpu.VMEM((n_rows, D_FF), bf16),
    ]
    y, *new_states = pl.pallas_call(
        functools.partial(_layer_kernel, ns, ni, n_t, chained, skew, start_pos, apply_ln0),
        grid=(1, n_steps),
        in_specs=in_specs,
        out_specs=out_specs,
        out_shape=out_shape,
        scratch_shapes=scratch,
        compiler_params=pltpu.CompilerParams(
            dimension_semantics=("arbitrary", "arbitrary"),
            vmem_limit_bytes=VMEM_LIMIT_BYTES),
        name=name,
    )(*operands)
    return y.reshape(x.shape), new_states


def _block_diag_gates(w):
    hd = D_LRU // LRU_HEADS
    per = MXU_DIM // hd
    w4 = w.reshape(D_LRU // MXU_DIM, per, hd, hd)
    eye = jnp.eye(per, dtype=w.dtype)
    return jnp.einsum('xhde,hg->xhdge', w4, eye).reshape(D_LRU // MXU_DIM, MXU_DIM, MXU_DIM).astype(jnp.bfloat16)


def _rep8(w):
    return jnp.broadcast_to(w[:, None, :], (w.shape[0], SUBLANES, w.shape[1]))


def _layer_weights(l, w_in, b_in, conv_dw_w, conv_dw_b, conv_gn_g, conv_gn_b, lru_conv_w, lru_conv_b,
                   lru_wa, lru_ba, lru_wx, lru_bx, lru_lambda, w_out, b_out, ln1_g, ln1_b,
                   ffn_w_up, ffn_conv_w, ffn_conv_b, ffn_w_down, ln2_g, ln2_b):
    bf16 = jnp.bfloat16
    row = lambda v: v.reshape(1, -1)
    grp = D_CONV // CONV_GROUPS
    gidx = jnp.arange(MXU_DIM) // grp
    gmat = jnp.where(gidx[:, None] == gidx[None, :], 1.0 / grp, 0.0).astype(bf16)
    return (
        w_in[l].astype(bf16), row(b_in[l]),
        _rep8(conv_dw_w[l]), row(conv_dw_b[l]), row(conv_gn_g[l]), row(conv_gn_b[l]), gmat,
        lru_conv_w[l], row(lru_conv_b[l]),
        _block_diag_gates(lru_wa[l]), _block_diag_gates(lru_wx[l]),
        row(lru_ba[l]), row(lru_bx[l]), row(lru_lambda[l]),
        w_out[l].astype(bf16), row(b_out[l]), row(ln1_g[l]), row(ln1_b[l]),
        ffn_w_up[l].astype(bf16), ffn_conv_w[l], row(ffn_conv_b[l]),
        ffn_w_down[l].astype(bf16), row(ln2_g[l]), row(ln2_b[l]),
    )


def kernel(x_prompt, x_sample, state_conv_dw, state_lru_conv, state_lru_h, state_ffn_conv, ln0_g, ln0_b, w_in, b_in, conv_dw_w, conv_dw_b, conv_gn_g, conv_gn_b, lru_conv_w, lru_conv_b, lru_wa, lru_ba, lru_wx, lru_bx, lru_lambda, w_out, b_out, ln1_g, ln1_b, ffn_w_up, ffn_conv_w, ffn_conv_b, ffn_w_down, ln2_g, ln2_b):
    f32 = jnp.float32
    nb, seq, _ = x_prompt.shape
    db, dseq, _ = x_sample.shape
    ln0 = (ln0_g.reshape(1, -1), ln0_b.reshape(1, -1))
    tile = PROMPT_NS * PROMPT_NI
    p_in = (jnp.zeros((nb, 1, (CONV_W - 1) * D_CONV), f32), jnp.zeros((nb, 1, (LRU_CONV_W - 1) * D_LRU), f32),
            jnp.zeros((nb, 1, D_LRU), f32), jnp.zeros((nb, 1, (FFN_CONV_W - 1) * D_FF), f32))
    xp = x_prompt.reshape(nb, seq // tile, PROMPT_NS, PROMPT_NI * D_MODEL)
    xs = x_sample.reshape(1, 1, db, dseq * D_MODEL)
    p_states, s_states = [], []
    for l in range(DEPTH):
        wts = _layer_weights(l, w_in, b_in, conv_dw_w, conv_dw_b, conv_gn_g, conv_gn_b, lru_conv_w, lru_conv_b,
                             lru_wa, lru_ba, lru_wx, lru_bx, lru_lambda, w_out, b_out, ln1_g, ln1_b,
                             ffn_w_up, ffn_conv_w, ffn_conv_b, ffn_w_down, ln2_g, ln2_b)
        xp, pst = _run_layer(xp, p_in, wts, ln0, PROMPT_NS, PROMPT_NI, True, True, 0, l == 0, f"prompt_l{l}")
        p_states.append(pst)
        s_in = (state_conv_dw[l].reshape(1, db, -1), state_lru_conv[l].reshape(1, db, -1),
                state_lru_h[l].reshape(1, db, D_LRU), state_ffn_conv[l].reshape(1, db, -1))
        xs, sst = _run_layer(xs, s_in, wts, ln0, db, dseq, False, False, PAST_LEN, l == 0, f"sample_l{l}")
        s_states.append(sst)

    def stack(states, k, shape):
        return jnp.stack([st[k] for st in states]).reshape((DEPTH,) + shape)

    return (xp.reshape(nb, seq, D_MODEL), xs.reshape(db, dseq, D_MODEL),
            stack(p_states, 0, (nb, CONV_W - 1, D_CONV)), stack(p_states, 1, (nb, LRU_CONV_W - 1, D_LRU)),
            stack(p_states, 2, (nb, D_LRU)), stack(p_states, 3, (nb, FFN_CONV_W - 1, D_FF)),
            stack(s_states, 0, (db, CONV_W - 1, D_CONV)), stack(s_states, 1, (db, LRU_CONV_W - 1, D_LRU)),
            stack(s_states, 2, (db, D_LRU)), stack(s_states, 3, (db, FFN_CONV_W - 1, D_FF)))
```
